```python
import math, functools
import jax, jax.numpy as jnp
from jax import lax
import numpy as np

D_MODEL = 2048
BATCH = 4
SEQ = 2048
DEPTH = 1
DEC_BATCH = 32
DEC_SEQ = 4
PAST_LEN = 16384
PAGE_SIZE = 128

N_HEADS = 8
HEAD_DK = 64
HEAD_DV = 2 * HEAD_DK
QK_WIDTH = N_HEADS * 2 * HEAD_DK
V_WIDTH = N_HEADS * HEAD_DV
Q_BLOCK = 128
N_BUCKETS = 32
MAX_DISTANCE = 128
D_RNN = D_MODEL // 2
N_RNN_BLOCKS = 8
RNN_BLOCK = D_RNN // N_RNN_BLOCKS
CONV_W = 4
RG_C = 8.0
N_EXPERTS = 32
TOP_K = 4
D_FF = D_MODEL
SWIGLU_LIMIT = 7.0
SWIGLU_ALPHA = 1.702
EXPERT_BLOCK = 128
NORM_EPS = 1e-6
IN_SPLITS = [QK_WIDTH, 2 * QK_WIDTH, 2 * QK_WIDTH + V_WIDTH, 2 * QK_WIDTH + V_WIDTH + D_RNN,
             2 * QK_WIDTH + V_WIDTH + 2 * D_RNN, 2 * QK_WIDTH + V_WIDTH + 2 * D_RNN + D_MODEL]
IN_WIDTH = 2 * QK_WIDTH + V_WIDTH + 2 * D_RNN + 2 * D_MODEL

kernel_name = "hybrid_diffattn_rglru_moe_step"

F32 = jnp.float32


def rms_norm(x, g):
    xf = x.astype(F32)
    y = xf * lax.rsqrt(jnp.mean(xf * xf, axis=-1, keepdims=True) + NORM_EPS)
    return (y * g.astype(F32)).astype(x.dtype)


def adaln(c, w, b):
    mod = jax.nn.silu(c) @ w + b
    return [m[:, None, :] for m in jnp.split(mod, 6, axis=-1)]


def t5_bucket(dist):
    n = jnp.maximum(dist, 0)
    max_exact = N_BUCKETS // 2
    nf = jnp.maximum(n, 1).astype(F32)
    large = max_exact + (jnp.log(nf / max_exact) / math.log(MAX_DISTANCE / max_exact)
                         * (N_BUCKETS - max_exact)).astype(jnp.int32)
    large = jnp.minimum(large, N_BUCKETS - 1)
    return jnp.where(n < max_exact, n, large)


def diff_scores(q, k, dist, rel_bias):
    s = jnp.einsum("bthmd,bkhmd->mbhtk", q, k).astype(F32) * (HEAD_DK ** -0.5)
    bias = jnp.transpose(rel_bias[t5_bucket(dist)].astype(F32), (2, 0, 1))
    return s + bias[None, None]


def online_update(m, l, acc, sc, v):
    m_new = jnp.maximum(m, sc.max(-1))
    corr = jnp.exp(m - m_new)
    p = jnp.exp(sc - m_new[..., None])
    l = l * corr + p.sum(-1)
    acc = acc * corr[..., None] + jnp.einsum("mbhtk,bkhd->mbhtd", p, v)
    return m_new, l, acc


def prompt_diff_attention(q, k, v, lam, rel_bias):
    b, s = q.shape[:2]
    n_blk = s // Q_BLOCK
    q_blocks = jnp.moveaxis(q.reshape(b, n_blk, Q_BLOCK, N_HEADS, 2, HEAD_DK), 1, 0)
    k_pos = jnp.arange(s)
    vf = v.astype(F32)

    def one_block(args):
        q_blk, i = args
        q_pos = i * Q_BLOCK + jnp.arange(Q_BLOCK)
        dist = q_pos[:, None] - k_pos[None, :]
        sc = jnp.where(dist >= 0, diff_scores(q_blk, k, dist, rel_bias), -jnp.inf)
        p = jax.nn.softmax(sc, axis=-1)
        return jnp.einsum("bhtk,bkhd->bthd", p[0] - lam * p[1], vf)

    o = lax.map(one_block, (q_blocks, jnp.arange(n_blk)))
    return jnp.moveaxis(o, 0, 1).reshape(b, s, N_HEADS, HEAD_DV)


def sample_diff_attention(q, k, v, lam, cache_k, cache_v, page_table, rel_bias, layer):
    b, t = q.shape[:2]
    q_pos = PAST_LEN + jnp.arange(t)

    def page_step(carry, xs):
        p_idx, phys = xs
        k_pg = cache_k[layer, phys].reshape(b, PAGE_SIZE, N_HEADS, 2, HEAD_DK)
        v_pg = cache_v[layer, phys].astype(F32)
        k_pos = p_idx * PAGE_SIZE + jnp.arange(PAGE_SIZE)
        sc = diff_scores(q, k_pg, q_pos[:, None] - k_pos[None, :], rel_bias)
        return online_update(*carry, sc, v_pg), None

    init = (jnp.full((2, b, N_HEADS, t), -jnp.inf, F32),
            jnp.zeros((2, b, N_HEADS, t), F32),
            jnp.zeros((2, b, N_HEADS, t, HEAD_DV), F32))
    n_pages = page_table.shape[1]
    carry, _ = lax.scan(page_step, init, (jnp.arange(n_pages), page_table.T))
    dist = q_pos[:, None] - q_pos[None, :]
    sc = jnp.where(dist >= 0, diff_scores(q, k, dist, rel_bias), -jnp.inf)
    m, l, acc = online_update(*carry, sc, v.astype(F32))
    o = acc / l[..., None]
    o = o[0] - lam * o[1]
    return jnp.transpose(o, (0, 2, 1, 3))


def rglru_branch(xr, conv_buf, h0, w_conv, b_conv, w_rg, b_rg, w_ig, b_ig, lam_rg):
    b, t, _ = xr.shape
    buf = jnp.concatenate([conv_buf.astype(xr.dtype), xr], axis=1)
    xc = b_conv + sum(buf[:, j:j + t] * w_conv[j] for j in range(CONV_W))
    xf = xc.astype(F32)
    xb = xf.reshape(b, t, N_RNN_BLOCKS, RNN_BLOCK)
    r = jax.nn.sigmoid(jnp.einsum("btnc,ncd->btnd", xb, w_rg).reshape(b, t, D_RNN) + b_rg)
    i = jax.nn.sigmoid(jnp.einsum("btnc,ncd->btnd", xb, w_ig).reshape(b, t, D_RNN) + b_ig)
    log_a = -RG_C * jax.nn.softplus(-lam_rg.astype(F32)) * r
    a = jnp.exp(log_a)
    u = xf * i * jnp.sqrt(-jnp.expm1(2.0 * log_a))

    def step(h, au):
        a_t, u_t = au
        h = a_t * h + u_t
        return h, h

    h_last, hs = lax.scan(step, h0.astype(F32), (jnp.swapaxes(a, 0, 1), jnp.swapaxes(u, 0, 1)))
    return jnp.swapaxes(hs, 0, 1), buf[:, t:], h_last.astype(h0.dtype)


def moe_ffn(h, w_router, b_router, w_up, b_up, w_down, b_down):
    b, t, d = h.shape
    n = b * t
    nk = n * TOP_K
    hf = h.reshape(n, d)
    logits = (hf @ w_router + b_router).astype(F32)
    top_val, top_idx = lax.top_k(logits, TOP_K)
    gate = jax.nn.softmax(top_val, axis=-1)
    flat_e = top_idx.reshape(nk)
    order = jnp.argsort(flat_e)
    sorted_e = flat_e[order]
    counts = jnp.bincount(flat_e, length=N_EXPERTS)
    padded = (counts + EXPERT_BLOCK - 1) // EXPERT_BLOCK * EXPERT_BLOCK
    pad_end = jnp.cumsum(padded)
    start = jnp.cumsum(counts) - counts
    dest = (pad_end - padded)[sorted_e] + jnp.arange(nk) - start[sorted_e]
    n_blocks = -(-nk // EXPERT_BLOCK) + N_EXPERTS
    rows = jnp.zeros((n_blocks * EXPERT_BLOCK, d), h.dtype).at[dest].set(hf[order // TOP_K])
    block_expert = jnp.minimum(
        jnp.searchsorted(pad_end, jnp.arange(n_blocks) * EXPERT_BLOCK, side="right"), N_EXPERTS - 1)

    def expert_block(args):
        xb, e = args
        g, u = jnp.split(xb @ w_up[e] + b_up[e], 2, axis=-1)
        g = jnp.minimum(g, SWIGLU_LIMIT)
        u = jnp.clip(u, -SWIGLU_LIMIT, SWIGLU_LIMIT)
        return (g * jax.nn.sigmoid(SWIGLU_ALPHA * g) * (u + 1.0)) @ w_down[e] + b_down[e]

    out_rows = lax.map(expert_block, (rows.reshape(n_blocks, EXPERT_BLOCK, d), block_expert)).reshape(-1, d)
    y_slots = jnp.zeros((nk, d), h.dtype).at[order].set(out_rows[dest])
    y = jnp.einsum("nkd,nk->nd", y_slots.reshape(n, TOP_K, d).astype(F32), gate)
    return y.astype(h.dtype).reshape(b, t, d)


def decoder_layer(x, c, attend, conv_buf, h0, lam_init, w_ada, b_ada, norm_mix, norm_ffn, w_in,
                  lam_q1, lam_k1, lam_q2, lam_k2, subln_gain, w_conv, b_conv, w_rg, b_rg, w_ig, b_ig,
                  lam_rg, w_att_out, w_rnn_out, w_o, w_router, b_router, w_up, b_up, w_down, b_down):
    b, t, _ = x.shape
    sh1, sc1, g1, sh2, sc2, g2 = adaln(c, w_ada, b_ada)
    hn = rms_norm(x, norm_mix) * (1.0 + sc1) + sh1
    q, k, v, xr, gr, ga, gb = jnp.split(hn @ w_in, IN_SPLITS, axis=-1)
    q = q.reshape(b, t, N_HEADS, 2, HEAD_DK)
    k = k.reshape(b, t, N_HEADS, 2, HEAD_DK)
    v = v.reshape(b, t, N_HEADS, HEAD_DV)
    lam = (jnp.exp(jnp.sum(lam_q1.astype(F32) * lam_k1)) - jnp.exp(jnp.sum(lam_q2.astype(F32) * lam_k2))
           + lam_init)
    o_att = rms_norm(attend(q, k, v, lam), subln_gain) * (1.0 - lam_init)
    y_att = o_att.reshape(b, t, V_WIDTH).astype(x.dtype) @ w_att_out
    hs, conv_new, h_new = rglru_branch(xr, conv_buf, h0, w_conv, b_conv, w_rg, b_rg, w_ig, b_ig, lam_rg)
    y_rnn = (hs.astype(x.dtype) * jax.nn.gelu(gr)) @ w_rnn_out
    mixed = jax.nn.sigmoid(ga) * y_att + jax.nn.sigmoid(gb) * y_rnn
    x = x + g1 * (mixed @ w_o)
    hn2 = rms_norm(x, norm_ffn) * (1.0 + sc2) + sh2
    x = x + g2 * moe_ffn(hn2, w_router, b_router, w_up, b_up, w_down, b_down)
    return x, k.reshape(b, t, N_HEADS, 2 * HEAD_DK), v, conv_new, h_new


def setup_inputs(seed: int = 0) -> dict:
    key = jax.random.key(seed)
    ks = iter(jax.random.split(key, 48))

    def nrm(shape, scale):
        return jax.random.normal(next(ks), shape, F32) * scale

    n_pages = PAST_LEN // PAGE_SIZE
    n_used = DEC_BATCH * n_pages
    n_pool = n_used + max(1, n_used // 4)
    page_table = jax.random.permutation(next(ks), n_pool)[:n_used].reshape(DEC_BATCH, n_pages).astype(jnp.int32)
    u = jax.random.uniform(next(ks), (DEPTH, D_RNN), F32, 0.9, 0.999)
    a0 = u ** (1.0 / RG_C)
    lam_rg = jnp.log(a0) - jnp.log1p(-a0)
    return {
        "x_prompt": nrm((BATCH, SEQ, D_MODEL), 1.0),
        "x_sample": nrm((DEC_BATCH, DEC_SEQ, D_MODEL), 1.0),
        "c_prompt": nrm((BATCH, D_MODEL), 1.0),
        "c_sample": nrm((DEC_BATCH, D_MODEL), 1.0),
        "cache_k": nrm((DEPTH, n_pool, PAGE_SIZE, N_HEADS, 2 * HEAD_DK), 1.0),
        "cache_v": nrm((DEPTH, n_pool, PAGE_SIZE, N_HEADS, HEAD_DV), 1.0),
        "state_conv": nrm((DEPTH, DEC_BATCH, CONV_W - 1, D_RNN), 1.0),
        "state_h": nrm((DEPTH, DEC_BATCH, D_RNN), 0.5),
        "page_table": page_table,
        "rel_bias": nrm((N_BUCKETS, N_HEADS), 0.5),
        "w_ada": nrm((DEPTH, D_MODEL, 6 * D_MODEL), D_MODEL ** -0.5),
        "b_ada": nrm((DEPTH, 6 * D_MODEL), 0.02),
        "norm_mix": 1.0 + nrm((DEPTH, D_MODEL), 0.05),
        "norm_ffn": 1.0 + nrm((DEPTH, D_MODEL), 0.05),
        "w_in": nrm((DEPTH, D_MODEL, IN_WIDTH), D_MODEL ** -0.5),
        "lam_q1": nrm((DEPTH, HEAD_DK), 0.1),
        "lam_k1": nrm((DEPTH, HEAD_DK), 0.1),
        "lam_q2": nrm((DEPTH, HEAD_DK), 0.1),
        "lam_k2": nrm((DEPTH, HEAD_DK), 0.1),
        "subln_gain": 1.0 + nrm((DEPTH, HEAD_DV), 0.05),
        "w_conv": nrm((DEPTH, CONV_W, D_RNN), CONV_W ** -0.5),
        "b_conv": nrm((DEPTH, D_RNN), 0.02),
        "w_rg": nrm((DEPTH, N_RNN_BLOCKS, RNN_BLOCK, RNN_BLOCK), RNN_BLOCK ** -0.5),
        "b_rg": nrm((DEPTH, D_RNN), 0.02),
        "w_ig": nrm((DEPTH, N_RNN_BLOCKS, RNN_BLOCK, RNN_BLOCK), RNN_BLOCK ** -0.5),
        "b_ig": nrm((DEPTH, D_RNN), 0.02),
        "lam_rg": lam_rg,
        "w_att_out": nrm((DEPTH, V_WIDTH, D_MODEL), V_WIDTH ** -0.5),
        "w_rnn_out": nrm((DEPTH, D_RNN, D_MODEL), D_RNN ** -0.5),
        "w_o": nrm((DEPTH, D_MODEL, D_MODEL), D_MODEL ** -0.5),
        "w_router": nrm((DEPTH, D_MODEL, N_EXPERTS), D_MODEL ** -0.5),
        "b_router": nrm((DEPTH, N_EXPERTS), 0.01),
        "w_up": nrm((DEPTH, N_EXPERTS, D_MODEL, 2 * D_FF), D_MODEL ** -0.5),
        "b_up": nrm((DEPTH, N_EXPERTS, 2 * D_FF), 0.02),
        "w_down": nrm((DEPTH, N_EXPERTS, D_FF, D_MODEL), D_FF ** -0.5),
        "b_down": nrm((DEPTH, N_EXPERTS, D_MODEL), 0.02),
        "final_norm": 1.0 + nrm((D_MODEL,), 0.05),
    }


def reference(x_prompt, x_sample, c_prompt, c_sample, cache_k, cache_v, state_conv, state_h, page_table,
              rel_bias, w_ada, b_ada, norm_mix, norm_ffn, w_in, lam_q1, lam_k1, lam_q2, lam_k2, subln_gain,
              w_conv, b_conv, w_rg, b_rg, w_ig, b_ig, lam_rg, w_att_out, w_rnn_out, w_o, w_router, b_router,
              w_up, b_up, w_down, b_down, final_norm):
    xp, xs = x_prompt, x_sample
    bp = x_prompt.shape[0]
    kp_l, vp_l, cp_l, hp_l, ks_l, vs_l, cs_l, hs_l = [], [], [], [], [], [], [], []
    attend_p = functools.partial(prompt_diff_attention, rel_bias=rel_bias)
    for l in range(DEPTH):
        lam_init = 0.8 - 0.6 * math.exp(-0.3 * l)
        lw = (w_ada[l], b_ada[l], norm_mix[l], norm_ffn[l], w_in[l], lam_q1[l], lam_k1[l], lam_q2[l],
              lam_k2[l], subln_gain[l], w_conv[l], b_conv[l], w_rg[l], b_rg[l], w_ig[l], b_ig[l], lam_rg[l],
              w_att_out[l], w_rnn_out[l], w_o[l], w_router[l], b_router[l], w_up[l], b_up[l], w_down[l],
              b_down[l])
        conv0 = jnp.zeros((bp, CONV_W - 1, D_RNN), x_prompt.dtype)
        h0 = jnp.zeros((bp, D_RNN), x_prompt.dtype)
        xp, kp, vp, cp, hp = decoder_layer(xp, c_prompt, attend_p, conv0, h0, lam_init, *lw)
        attend_s = functools.partial(sample_diff_attention, cache_k=cache_k, cache_v=cache_v,
                                     page_table=page_table, rel_bias=rel_bias, layer=l)
        xs, ks, vs, cs, hs = decoder_layer(xs, c_sample, attend_s, state_conv[l], state_h[l], lam_init, *lw)
        kp_l.append(kp); vp_l.append(vp); cp_l.append(cp); hp_l.append(hp)
        ks_l.append(ks); vs_l.append(vs); cs_l.append(cs); hs_l.append(hs)
    y_prompt = rms_norm(xp, final_norm)
    y_sample = rms_norm(xs, final_norm)
    return (y_prompt, y_sample, jnp.stack(kp_l), jnp.stack(vp_l), jnp.stack(cp_l), jnp.stack(hp_l),
            jnp.stack(ks_l), jnp.stack(vs_l), jnp.stack(cs_l), jnp.stack(hs_l))
```

```python
import functools
import math

import jax
import jax.numpy as jnp
from jax import lax
from jax.experimental import pallas as pl
from jax.experimental.pallas import tpu as pltpu

F32, BF16, I32 = jnp.float32, jnp.bfloat16, jnp.int32

TOP_K = 4
RG_C = 8.0
SWIGLU_LIMIT = 7.0
SWIGLU_ALPHA = 1.702
NORM_EPS = 1e-6
MAX_DISTANCE = 128
PAGE_SIZE = 128
MASKED = -1e30

V7X_VMEM_LIMIT_BYTES = 56 * 1024 * 1024
SUBLANES = 8

PROMPT_ROW_TILE = 512
WEIGHT_COL_TILE = 1024
EXPERT_ROW_TILE = 512
FF_TILE = 512

NT_DIMS = (((1,), (1,)), ((), ()))


def _cparams(*sem):
    return pltpu.CompilerParams(dimension_semantics=sem, vmem_limit_bytes=V7X_VMEM_LIMIT_BYTES)


def _rms(x, gain):
    return x * lax.rsqrt(jnp.mean(x * x, axis=-1, keepdims=True) + NORM_EPS) * gain


def _gelu_tanh(x):
    return 0.5 * x * (1.0 + jnp.tanh(math.sqrt(2.0 / math.pi) * (x + 0.044715 * (x * x * x))))


def _softplus(x):
    return jnp.maximum(x, 0.0) + jnp.log1p(jnp.exp(-jnp.abs(x)))


def _t5_bucket(dist, n_buckets):
    n = jnp.maximum(dist, 0)
    max_exact = n_buckets // 2
    nf = jnp.maximum(n, 1).astype(F32)
    large = max_exact + (jnp.log(nf / max_exact) / math.log(MAX_DISTANCE / max_exact)
                         * (n_buckets - max_exact)).astype(I32)
    large = jnp.minimum(large, n_buckets - 1)
    return jnp.where(n < max_exact, n, large)


def _diff_lambda(lam_ref, lam_init):
    lq = lam_ref[...]
    s1 = jnp.sum(lq[0:1] * lq[1:2], axis=-1, keepdims=True)
    s2 = jnp.sum(lq[2:3] * lq[3:4], axis=-1, keepdims=True)
    return jnp.exp(s1) - jnp.exp(s2) + lam_init


def _ada_kernel(c_ref, w_ref, b_ref, o_ref):
    c = c_ref[...]
    s = (c * jax.nn.sigmoid(c)).astype(BF16)
    o_ref[...] = jnp.dot(s, w_ref[...].astype(BF16), preferred_element_type=F32) + b_ref[...]


def _ada(c, w, b, tn=WEIGHT_COL_TILE):
    m, d = c.shape
    n = w.shape[1]
    return pl.pallas_call(
        _ada_kernel,
        grid=(n // tn,),
        in_specs=[pl.BlockSpec((m, d), lambda j: (0, 0)),
                  pl.BlockSpec((d, tn), lambda j: (0, j)),
                  pl.BlockSpec((1, tn), lambda j: (0, j))],
        out_specs=pl.BlockSpec((m, tn), lambda j: (0, j)),
        out_shape=jax.ShapeDtypeStruct((m, n), F32),
        compiler_params=_cparams("arbitrary"),
        name="adaln",
    )(c, w, b.reshape(1, n))


def _prenorm_kernel(x_ref, g_ref, sc_ref, sh_ref, o_ref):
    y = _rms(x_ref[...], g_ref[...])
    o_ref[...] = (y * (1.0 + sc_ref[...]) + sh_ref[...]).astype(o_ref.dtype)


def _prenorm(x, gain, sc, sh, tm, tiles_per_group):
    r, d = x.shape
    rm = sc.shape[1]
    mod_spec = pl.BlockSpec((None, rm, d), lambda i: (i // tiles_per_group, 0, 0))
    return pl.pallas_call(
        _prenorm_kernel,
        grid=(r // tm,),
        in_specs=[pl.BlockSpec((tm, d), lambda i: (i, 0)),
                  pl.BlockSpec((1, d), lambda i: (0, 0)), mod_spec, mod_spec],
        out_specs=pl.BlockSpec((tm, d), lambda i: (i, 0)),
        out_shape=jax.ShapeDtypeStruct((r, d), BF16),
        compiler_params=_cparams("arbitrary"),
        name="prenorm_mix",
    )(x, gain.reshape(1, d), sc, sh)


def _proj_kernel(a_ref, w_ref, *refs, scale):
    outs, wbf = refs[:-1], refs[-1]

    @pl.when(pl.program_id(1) == 0)
    def _():
        wbf[...] = w_ref[...].astype(BF16)

    acc = jnp.dot(a_ref[...], wbf[...], preferred_element_type=F32)
    if scale != 1.0:
        acc = acc * scale
    for o in outs:
        o[...] = acc.astype(o.dtype)


def _proj(a, w, col_off, n_cols, out_dtypes, tm, tn, scale=1.0, name="proj"):
    m, k = a.shape
    off = col_off // tn
    out_spec = pl.BlockSpec((tm, tn), lambda j, i: (i, j))
    outs = pl.pallas_call(
        functools.partial(_proj_kernel, scale=scale),
        grid=(n_cols // tn, m // tm),
        in_specs=[pl.BlockSpec((tm, k), lambda j, i: (i, 0)),
                  pl.BlockSpec((k, tn), lambda j, i: (0, j + off))],
        out_specs=[out_spec] * len(out_dtypes),
        out_shape=[jax.ShapeDtypeStruct((m, n_cols), dt) for dt in out_dtypes],
        scratch_shapes=[pltpu.VMEM((k, tn), BF16)],
        compiler_params=_cparams("arbitrary", "arbitrary"),
        name=name,
    )(a, w)
    return outs


def _mix_kernel(a1_ref, a2_ref, w1_ref, w2_ref, ga_ref, gb_ref, o_ref, w1bf, w2bf):
    @pl.when(pl.program_id(1) == 0)
    def _():
        w1bf[...] = w1_ref[...].astype(BF16)
        w2bf[...] = w2_ref[...].astype(BF16)

    y1 = jnp.dot(a1_ref[...], w1bf[...], preferred_element_type=F32)
    y2 = jnp.dot(a2_ref[...], w2bf[...], preferred_element_type=F32)
    o_ref[...] = (jax.nn.sigmoid(ga_ref[...]) * y1 + jax.nn.sigmoid(gb_ref[...]) * y2).astype(o_ref.dtype)


def _mix(a1, a2, w1, w2, gates, tm, tn):
    m, k = a1.shape
    n = w1.shape[1]
    nb = n // tn
    return pl.pallas_call(
        _mix_kernel,
        grid=(nb, m // tm),
        in_specs=[pl.BlockSpec((tm, k), lambda j, i: (i, 0)),
                  pl.BlockSpec((tm, k), lambda j, i: (i, 0)),
                  pl.BlockSpec((k, tn), lambda j, i: (0, j)),
                  pl.BlockSpec((k, tn), lambda j, i: (0, j)),
                  pl.BlockSpec((tm, tn), lambda j, i: (i, j)),
                  pl.BlockSpec((tm, tn), lambda j, i: (i, j + nb))],
        out_specs=pl.BlockSpec((tm, tn), lambda j, i: (i, j)),
        out_shape=jax.ShapeDtypeStruct((m, n), BF16),
        scratch_shapes=[pltpu.VMEM((k, tn), BF16), pltpu.VMEM((k, tn), BF16)],
        compiler_params=_cparams("arbitrary", "arbitrary"),
        name="branch_mix",
    )(a1, a2, w1, w2, gates, gates)


def _resid_kernel(a_ref, w_ref, x_ref, g_ref, o_ref, wbf):
    @pl.when(pl.program_id(1) == 0)
    def _():
        wbf[...] = w_ref[...].astype(BF16)

    y = jnp.dot(a_ref[...], wbf[...], preferred_element_type=F32)
    o_ref[...] = x_ref[...] + g_ref[...] * y


def _resid(a, w, x, g, tm, tn, tiles_per_group):
    m, k = a.shape
    n = w.shape[1]
    rm = g.shape[1]
    return pl.pallas_call(
        _resid_kernel,
        grid=(n // tn, m // tm),
        in_specs=[pl.BlockSpec((tm, k), lambda j, i: (i, 0)),
                  pl.BlockSpec((k, tn), lambda j, i: (0, j)),
                  pl.BlockSpec((tm, tn), lambda j, i: (i, j)),
                  pl.BlockSpec((None, rm, tn), lambda j, i: (i // tiles_per_group, 0, j))],
        out_specs=pl.BlockSpec((tm, tn), lambda j, i: (i, j)),
        out_shape=jax.ShapeDtypeStruct((m, n), F32),
        scratch_shapes=[pltpu.VMEM((k, tn), BF16)],
        compiler_params=_cparams("arbitrary", "arbitrary"),
        name="out_proj_residual",
    )(a, w, x, g)


def _pattn_kernel(rb_ref, lam_ref, gain_ref, q_ref, k_ref, v_ref, o_ref, bias_s, m_s, l_s, acc_s,
                  *, blk, lam_init, n_buckets):
    h = pl.program_id(1)
    qi = pl.program_id(2)
    dk = q_ref.shape[1] // 2

    @pl.when(qi == 0)
    def _():
        r = lax.broadcasted_iota(I32, (blk, blk), 0)
        c = lax.broadcasted_iota(I32, (blk, blk), 1)
        far = rb_ref[h, n_buckets - 1]
        for which in range(2):
            dist = r - c + which * blk
            bkt = _t5_bucket(dist, n_buckets)
            val = jnp.zeros((blk, blk), F32)
            for b in range(n_buckets):
                val = jnp.where(bkt == b, rb_ref[h, b], val)
            val = val - far
            if which == 0:
                val = jnp.where(dist >= 0, val, MASKED)
            bias_s[which] = val

    q = q_ref[...]
    lane = lax.broadcasted_iota(I32, q.shape, 1)
    qmaps = (jnp.where(lane < dk, q, jnp.zeros_like(q)), jnp.where(lane >= dk, q, jnp.zeros_like(q)))

    m_s[...] = jnp.full(m_s.shape, -jnp.inf, F32)
    l_s[...] = jnp.zeros(l_s.shape, F32)
    acc_s[...] = jnp.zeros(acc_s.shape, F32)

    def process(ki, bias):
        start = pl.multiple_of(ki * blk, blk)
        k = k_ref[pl.ds(start, blk), :]
        v = v_ref[pl.ds(start, blk), :]
        for mp in range(2):
            s = lax.dot_general(qmaps[mp], k, NT_DIMS, preferred_element_type=F32)
            if bias is not None:
                s = s + bias
            m_old = m_s[mp]
            m_new = jnp.maximum(m_old, jnp.max(s, axis=-1, keepdims=True))
            p = jnp.exp(s - m_new)
            corr = jnp.exp(m_old - m_new)
            l_s[mp] = l_s[mp] * corr + jnp.sum(p, axis=-1, keepdims=True)
            acc_s[mp] = acc_s[mp] * corr + jnp.dot(p.astype(BF16), v, preferred_element_type=F32)
            m_s[mp] = m_new

    @pl.loop(0, jnp.maximum(qi - 1, 0))
    def _(ki):
        process(ki, None)

    @pl.when(qi >= 1)
    def _():
        process(qi - 1, bias_s[1])

    process(qi, bias_s[0])

    lam = _diff_lambda(lam_ref, lam_init)
    o = acc_s[0] / l_s[0] - lam * (acc_s[1] / l_s[1])
    o_ref[...] = (_rms(o, gain_ref[...]) * (1.0 - lam_init)).astype(o_ref.dtype)


def _prompt_attention(q, k, v, rel_bias_t, lam_vecs, gain, batch, seq, lam_init, blk=256):
    n_heads, n_buckets = rel_bias_t.shape
    dv = v.shape[1] // n_heads
    assert blk >= MAX_DISTANCE and seq % blk == 0
    nq = seq // blk
    return pl.pallas_call(
        functools.partial(_pattn_kernel, blk=blk, lam_init=lam_init, n_buckets=n_buckets),
        grid=(batch, n_heads, nq),
        in_specs=[pl.BlockSpec(memory_space=pltpu.SMEM),
                  pl.BlockSpec(lam_vecs.shape, lambda b, h, i: (0, 0)),
                  pl.BlockSpec((1, dv), lambda b, h, i: (0, 0)),
                  pl.BlockSpec((blk, dv), lambda b, h, i: (b * nq + i, h)),
                  pl.BlockSpec((seq, dv), lambda b, h, i: (b, h)),
                  pl.BlockSpec((seq, dv), lambda b, h, i: (b, h))],
        out_specs=pl.BlockSpec((blk, dv), lambda b, h, i: (b * nq + i, h)),
        out_shape=jax.ShapeDtypeStruct(v.shape, BF16),
        scratch_shapes=[pltpu.VMEM((2, blk, blk), F32), pltpu.VMEM((2, blk, 1), F32),
                        pltpu.VMEM((2, blk, 1), F32), pltpu.VMEM((2, blk, dv), F32)],
        compiler_params=_cparams("arbitrary", "arbitrary", "arbitrary"),
        name="prompt_diff_attention",
    )(rel_bias_t, lam_vecs, gain.reshape(1, dv), q, k, v)


def _sattn_kernel(pt_ref, rbt_ref, lam_ref, gain_ref, qb_ref, ks_ref, vs_ref, ck_hbm, cv_hbm, o_ref,
                  kbuf, vbuf, mb_s, mbs_s, m_s, l_s, acc_s, sem,
                  *, group, n_pages, n_batch, page_off, lam_init, n_buckets, n_heads, t_new):
    b = pl.program_id(0)
    n_groups = n_pages // group
    rows = qb_ref.shape[0]
    half = rows // 2
    cols = kbuf.shape[2]
    scols = ks_ref.shape[0]

    @pl.when(b == 0)
    def _():
        def lookup(dist, width):
            bkt = _t5_bucket(dist, n_buckets)
            val = jnp.zeros(dist.shape, F32)
            for i in range(n_buckets):
                val = jnp.where(bkt == i, rbt_ref[i:i + 1, 0:width], val)
            return val - rbt_ref[n_buckets - 1:n_buckets, 0:width]

        r = lax.broadcasted_iota(I32, (rows, cols), 0)
        c = lax.broadcasted_iota(I32, (rows, cols), 1)
        hrow = (r % half) // t_new
        trow = r % t_new
        valid = hrow == c % n_heads
        mb_s[0] = jnp.where(valid, 0.0, MASKED)
        mb_s[1] = jnp.where(valid, lookup(PAGE_SIZE + trow - c // n_heads, cols), MASKED)
        r2 = lax.broadcasted_iota(I32, (rows, scols), 0)
        c2 = lax.broadcasted_iota(I32, (rows, scols), 1)
        t2 = r2 % t_new
        key2 = c2 // n_heads
        valid2 = ((r2 % half) // t_new == c2 % n_heads) & (key2 <= t2)
        mbs_s[...] = jnp.where(valid2, lookup(t2 - key2, scols), MASKED)

    def copies(bb, g, slot):
        out = []
        for j in range(group):
            page = pt_ref[bb, g * group + j] + page_off
            out.append(pltpu.make_async_copy(ck_hbm.at[page], kbuf.at[slot, j], sem.at[slot, 0, j]))
            out.append(pltpu.make_async_copy(cv_hbm.at[page], vbuf.at[slot, j], sem.at[slot, 1, j]))
        return out

    @pl.when(b == 0)
    def _():
        for cp in copies(0, 0, 0):
            cp.start()

    m_s[...] = jnp.full(m_s.shape, -jnp.inf, F32)
    l_s[...] = jnp.zeros(l_s.shape, F32)
    acc_s[...] = jnp.zeros(acc_s.shape, F32)
    qb = qb_ref[...]

    def update(kp, vp, mb):
        s = lax.dot_general(qb, kp, NT_DIMS, preferred_element_type=F32) + mb
        m_old = m_s[...]
        m_new = jnp.maximum(m_old, jnp.max(s, axis=-1, keepdims=True))
        p = jnp.exp(s - m_new)
        corr = jnp.exp(m_old - m_new)
        l_s[...] = l_s[...] * corr + jnp.sum(p, axis=-1, keepdims=True)
        acc_s[...] = acc_s[...] * corr + jnp.dot(p.astype(BF16), vp, preferred_element_type=F32)
        m_s[...] = m_new

    @pl.loop(0, n_groups)
    def _(g):
        slot = g % 2
        nxt = g + 1

        @pl.when(nxt < n_groups)
        def _():
            for cp in copies(b, nxt, 1 - slot):
                cp.start()

        @pl.when((nxt == n_groups) & (b + 1 < n_batch))
        def _():
            for cp in copies(b + 1, 0, 1 - slot):
                cp.start()

        for cp in copies(b, g, slot):
            cp.wait()
        for j in range(group):
            kp = kbuf[slot, j].astype(BF16)
            vp = vbuf[slot, j].astype(BF16)
            if j == group - 1:
                mb = mb_s[jnp.where(g == n_groups - 1, 1, 0)]
            else:
                mb = mb_s[0]
            update(kp, vp, mb)

    update(ks_ref[...], vs_ref[...], mbs_s[...])

    lam = _diff_lambda(lam_ref, lam_init)
    o = acc_s[...] / l_s[...]
    od = o[0:half] - lam * o[half:rows]
    o_ref[...] = (_rms(od, gain_ref[...]) * (1.0 - lam_init)).astype(o_ref.dtype)


def _sample_attention(qb, ks, vs, cache_k2, cache_v2, page_table, page_off, rbt, lam_vecs, gain, lam_init,
                      n_heads, t_new, group=8):
    n_batch, rows, dv = qb.shape
    n_pages = page_table.shape[1]
    cols = cache_k2.shape[1]
    n_buckets = rbt.shape[0]
    assert n_pages % group == 0 and (n_pages // group) % 2 == 0
    kern = functools.partial(_sattn_kernel, group=group, n_pages=n_pages, n_batch=n_batch, page_off=page_off,
                             lam_init=lam_init, n_buckets=n_buckets, n_heads=n_heads, t_new=t_new)
    grid_spec = pltpu.PrefetchScalarGridSpec(
        num_scalar_prefetch=1,
        grid=(n_batch,),
        in_specs=[pl.BlockSpec(rbt.shape, lambda b, pt: (0, 0)),
                  pl.BlockSpec(lam_vecs.shape, lambda b, pt: (0, 0)),
                  pl.BlockSpec((1, dv), lambda b, pt: (0, 0)),
                  pl.BlockSpec((None, rows, dv), lambda b, pt: (b, 0, 0)),
                  pl.BlockSpec((None,) + ks.shape[1:], lambda b, pt: (b, 0, 0)),
                  pl.BlockSpec((None,) + vs.shape[1:], lambda b, pt: (b, 0, 0)),
                  pl.BlockSpec(memory_space=pl.ANY),
                  pl.BlockSpec(memory_space=pl.ANY)],
        out_specs=pl.BlockSpec((None, rows // 2, dv), lambda b, pt: (b, 0, 0)),
        scratch_shapes=[pltpu.VMEM((2, group, cols, dv), F32), pltpu.VMEM((2, group, cols, dv), F32),
                        pltpu.VMEM((2, rows, cols), F32), pltpu.VMEM((rows, ks.shape[1]), F32),
                        pltpu.VMEM((rows, 1), F32), pltpu.VMEM((rows, 1), F32), pltpu.VMEM((rows, dv), F32),
                        pltpu.SemaphoreType.DMA((2, 2, group))])
    return pl.pallas_call(
        kern, grid_spec=grid_spec,
        out_shape=jax.ShapeDtypeStruct((n_batch, rows // 2, dv), BF16),
        compiler_params=_cparams("arbitrary"),
        name="sample_diff_attention",
    )(page_table, rbt, lam_vecs, gain.reshape(1, dv), qb, ks, vs, cache_k2, cache_v2)


def _rg_gates(xc, wrg_ref, brg_ref, wig_ref, big_ref, lam_ref):
    n_blocks, blk = wrg_ref.shape[0], wrg_ref.shape[1]
    rs, gs = [], []
    for n in range(n_blocks):
        xb = xc[:, n * blk:(n + 1) * blk].astype(BF16)
        rs.append(jnp.dot(xb, wrg_ref[n].astype(BF16), preferred_element_type=F32))
        gs.append(jnp.dot(xb, wig_ref[n].astype(BF16), preferred_element_type=F32))
    r = jax.nn.sigmoid(jnp.concatenate(rs, axis=-1) + brg_ref[...])
    i = jax.nn.sigmoid(jnp.concatenate(gs, axis=-1) + big_ref[...])
    log_a = (-RG_C * _softplus(-lam_ref[...])) * r
    a = jnp.exp(log_a)
    th = jnp.tanh(log_a)
    u = xc * i * jnp.sqrt(-2.0 * th / (1.0 - th))
    return a, u


def _rglru_prompt_kernel(x_ref, g_ref, wc_ref, bc_ref, wrg_ref, brg_ref, wig_ref, big_ref, lam_ref,
                         hr_ref, hl_ref, xbuf, a_s, u_s, h_s, *, conv_w):
    i = pl.program_id(1)
    tc = x_ref.shape[0]
    d = x_ref.shape[1]

    @pl.when(i == 0)
    def _():
        xbuf[0:SUBLANES] = jnp.zeros((SUBLANES, d), F32)
        h_s[...] = jnp.zeros(h_s.shape, F32)

    xbuf[SUBLANES:SUBLANES + tc] = x_ref[...]
    xc = bc_ref[...]
    for j in range(conv_w):
        lo = SUBLANES - (conv_w - 1) + j
        xc = xc + wc_ref[j:j + 1] * xbuf[lo:lo + tc]
    xbuf[0:SUBLANES] = xbuf[tc:tc + SUBLANES]

    a, u = _rg_gates(xc, wrg_ref, brg_ref, wig_ref, big_ref, lam_ref)
    a_s[...] = a
    u_s[...] = u
    row = lax.broadcasted_iota(I32, (SUBLANES, d), 0)

    @pl.loop(0, tc // SUBLANES)
    def _(g):
        st = pl.multiple_of(g * SUBLANES, SUBLANES)
        a8 = a_s[pl.ds(st, SUBLANES), :]
        u8 = u_s[pl.ds(st, SUBLANES), :]
        for s in (1, 2, 4):
            a_prev = jnp.where(row >= s, pltpu.roll(a8, s, 0), 1.0)
            u_prev = jnp.where(row >= s, pltpu.roll(u8, s, 0), 0.0)
            u8 = a8 * u_prev + u8
            a8 = a8 * a_prev
        h8 = a8 * h_s[...] + u8
        h_s[...] = jnp.broadcast_to(h8[SUBLANES - 1:SUBLANES], (SUBLANES, d))
        u_s[pl.ds(st, SUBLANES), :] = h8

    hr_ref[...] = (u_s[...] * _gelu_tanh(g_ref[...])).astype(hr_ref.dtype)

    @pl.when(i == pl.num_programs(1) - 1)
    def _():
        hl_ref[...] = h_s[0:1]


def _rglru_prompt(rg, w_conv, b_conv, w_rg, b_rg, w_ig, b_ig, lam_rg, batch, seq, tc=256):
    d = rg.shape[1] // 2
    nt = seq // tc
    conv_w = w_conv.shape[0]
    assert conv_w - 1 <= SUBLANES and seq % tc == 0
    vec = pl.BlockSpec((1, d), lambda b, i: (0, 0))
    wblk = pl.BlockSpec(w_rg.shape, lambda b, i: (0, 0, 0))
    return pl.pallas_call(
        functools.partial(_rglru_prompt_kernel, conv_w=conv_w),
        grid=(batch, nt),
        in_specs=[pl.BlockSpec((tc, d), lambda b, i: (b * nt + i, 0)),
                  pl.BlockSpec((tc, d), lambda b, i: (b * nt + i, 1)),
                  pl.BlockSpec((conv_w, d), lambda b, i: (0, 0)), vec, wblk, vec, wblk, vec, vec],
        out_specs=[pl.BlockSpec((tc, d), lambda b, i: (b * nt + i, 0)),
                   pl.BlockSpec((None, 1, d), lambda b, i: (b, 0, 0))],
        out_shape=[jax.ShapeDtypeStruct((batch * seq, d), BF16), jax.ShapeDtypeStruct((batch, 1, d), F32)],
        scratch_shapes=[pltpu.VMEM((tc + SUBLANES, d), F32), pltpu.VMEM((tc, d), F32),
                        pltpu.VMEM((tc, d), F32), pltpu.VMEM((SUBLANES, d), F32)],
        compiler_params=_cparams("arbitrary", "arbitrary"),
        name="rglru_prompt",
    )(rg, rg, w_conv, b_conv.reshape(1, d), w_rg, b_rg.reshape(1, d), w_ig, b_ig.reshape(1, d),
      lam_rg.reshape(1, d))


def _rglru_sample_kernel(x_ref, g_ref, cs_ref, h0_ref, wc_ref, bc_ref, wrg_ref, brg_ref, wig_ref, big_ref,
                         lam_ref, hr_ref, hl_ref, *, conv_w):
    t_new = x_ref.shape[0]
    buf = [cs_ref[j] for j in range(conv_w - 1)] + [x_ref[t] for t in range(t_new)]
    h = h0_ref[...]
    for t in range(t_new):
        xc = bc_ref[...]
        for j in range(conv_w):
            xc = xc + wc_ref[j:j + 1] * buf[t + j]
        a, u = _rg_gates(xc, wrg_ref, brg_ref, wig_ref, big_ref, lam_ref)
        h = a * h + u
        hr_ref[t] = (h * _gelu_tanh(g_ref[t])).astype(hr_ref.dtype)
    hl_ref[...] = h


def _rglru_sample(x_t, g_t, conv_t, h0, w_conv, b_conv, w_rg, b_rg, w_ig, b_ig, lam_rg):
    t_new, batch, d = x_t.shape
    return pl.pallas_call(
        functools.partial(_rglru_sample_kernel, conv_w=w_conv.shape[0]),
        out_shape=[jax.ShapeDtypeStruct((t_new, batch, d), BF16), jax.ShapeDtypeStruct((batch, d), F32)],
        compiler_params=pltpu.CompilerParams(vmem_limit_bytes=V7X_VMEM_LIMIT_BYTES),
        name="rglru_sample",
    )(x_t, g_t, conv_t, h0, w_conv, b_conv.reshape(1, d), w_rg, b_rg.reshape(1, d), w_ig, b_ig.reshape(1, d),
      lam_rg.reshape(1, d))


def _router_kernel(xp_ref, xs_ref, gain_ref, sc_ref, sh_ref, wr_ref, br_ref, hn_ref, idx_ref, gate_ref, rank_ref,
                   cnt_ref, base_s, *, n_prompt_tiles):
    i = pl.program_id(0)
    tm = xp_ref.shape[0]
    n_exp = wr_ref.shape[1]

    @pl.when(i == 0)
    def _():
        base_s[...] = jnp.zeros(base_s.shape, F32)

    x = jnp.where(i < n_prompt_tiles, xp_ref[...], xs_ref[...])
    hn = _rms(x, gain_ref[...]) * (1.0 + sc_ref[...]) + sh_ref[...]
    hn_ref[...] = hn
    logits = jnp.dot(hn, wr_ref[...], preferred_element_type=F32, precision=lax.Precision.HIGHEST) + br_ref[...]

    lane = lax.broadcasted_iota(I32, (tm, n_exp), 1)
    work = logits
    sels, vals = [], []
    for _ in range(TOP_K):
        mx = jnp.max(work, axis=-1, keepdims=True)
        first = jnp.min(jnp.where(work == mx, lane, n_exp), axis=-1, keepdims=True)
        sel = lane == first
        sels.append(sel)
        vals.append(mx)
        idx_ref[:, len(sels) - 1:len(sels)] = first
        work = jnp.where(sel, -jnp.inf, work)

    es = [jnp.exp(v - vals[0]) for v in vals]
    tot = es[0]
    for e in es[1:]:
        tot = tot + e
    for k in range(TOP_K):
        gate_ref[:, k:k + 1] = es[k] / tot

    chosen = sels[0]
    for sel in sels[1:]:
        chosen = chosen | sel
    onehot = jnp.where(chosen, 1.0, 0.0)
    r = lax.broadcasted_iota(I32, (tm, tm), 0)
    c = lax.broadcasted_iota(I32, (tm, tm), 1)
    tri = jnp.where(r > c, 1.0, 0.0).astype(BF16)
    pos = jnp.dot(tri, onehot.astype(BF16), preferred_element_type=F32) + base_s[...]
    for k in range(TOP_K):
        rank_ref[:, k:k + 1] = jnp.sum(jnp.where(sels[k], pos, 0.0), axis=-1, keepdims=True).astype(I32)
    base_s[...] = base_s[...] + jnp.sum(onehot, axis=0, keepdims=True)
    cnt_ref[...] = base_s[...].astype(I32)


def _router(x1p, x1s, gain, sc, sh, w_router, b_router, tiles_per_group):
    tm, d = x1s.shape
    npt = x1p.shape[0] // tm
    r = x1p.shape[0] + tm
    n_exp = w_router.shape[1]
    n_groups = sc.shape[0]
    mod_spec = pl.BlockSpec((None, tm, d), lambda i: (jnp.minimum(i // tiles_per_group, n_groups - 1), 0, 0))
    small = pl.BlockSpec((tm, TOP_K), lambda i: (i, 0))
    return pl.pallas_call(
        functools.partial(_router_kernel, n_prompt_tiles=npt),
        grid=(r // tm,),
        in_specs=[pl.BlockSpec((tm, d), lambda i: (jnp.minimum(i, npt - 1), 0)),
                  pl.BlockSpec((tm, d), lambda i: (0, 0)), pl.BlockSpec((1, d), lambda i: (0, 0)),
                  mod_spec, mod_spec,
                  pl.BlockSpec((d, n_exp), lambda i: (0, 0)), pl.BlockSpec((1, n_exp), lambda i: (0, 0))],
        out_specs=[pl.BlockSpec((tm, d), lambda i: (i, 0)), small, small, small,
                   pl.BlockSpec((1, n_exp), lambda i: (0, 0))],
        out_shape=[jax.ShapeDtypeStruct((r, d), F32), jax.ShapeDtypeStruct((r, TOP_K), I32),
                   jax.ShapeDtypeStruct((r, TOP_K), F32), jax.ShapeDtypeStruct((r, TOP_K), I32),
                   jax.ShapeDtypeStruct((1, n_exp), I32)],
        scratch_shapes=[pltpu.VMEM((1, n_exp), F32)],
        compiler_params=_cparams("arbitrary"),
        name="router_topk",
    )(x1p, x1s, gain.reshape(1, d), sc, sh, w_router, b_router.reshape(1, n_exp))


def _expert_kernel(te_ref, nu_ref, src_ref, hn_hbm, wg_ref, wu_ref, bg_ref, bu_ref, wd_ref, bd_ref, o_ref,
                   xg, xb, sem):
    t = pl.program_id(0)
    f = pl.program_id(1)
    tb = xb.shape[0]
    n_used = nu_ref[0]
    slot = t % 2

    def gather(tile, into):
        base = tile * tb

        @pl.loop(0, tb)
        def _(r):
            tok = src_ref[base + r]
            pltpu.make_async_copy(hn_hbm.at[pl.ds(tok, 1)], xg.at[into, pl.ds(r, 1)], sem.at[into]).start()

    @pl.when((t == 0) & (f == 0))
    def _():
        gather(0, 0)

    @pl.when((t < n_used) & (f == 0))
    def _():
        pltpu.make_async_copy(hn_hbm.at[pl.ds(0, tb)], xg.at[slot], sem.at[slot]).wait()
        xb[...] = xg[slot].astype(BF16)
        o_ref[...] = jnp.broadcast_to(bd_ref[...], o_ref.shape)

        @pl.when(t + 1 < n_used)
        def _():
            gather(t + 1, 1 - slot)

    @pl.when(t < n_used)
    def _():
        x = xb[...]
        g = jnp.dot(x, wg_ref[...].astype(BF16), preferred_element_type=F32) + bg_ref[...]
        u = jnp.dot(x, wu_ref[...].astype(BF16), preferred_element_type=F32) + bu_ref[...]
        g = jnp.minimum(g, SWIGLU_LIMIT)
        u = jnp.clip(u, -SWIGLU_LIMIT, SWIGLU_LIMIT)
        act = g * jax.nn.sigmoid(SWIGLU_ALPHA * g) * (u + 1.0)
        o_ref[...] += jnp.dot(act.astype(BF16), wd_ref[...].astype(BF16), preferred_element_type=F32)

    @pl.when((t >= n_used) & (f == 0))
    def _():
        o_ref[...] = jnp.zeros(o_ref.shape, F32)


def _experts(hn, tile_expert, n_used, src_token, w_up, b_up, w_down, b_down, tb, tf):
    n_exp, d, ff2 = w_up.shape
    ff = ff2 // 2
    nf = ff // tf
    n_tiles = tile_expert.shape[0]

    def eff(t, f, nu):
        return jnp.minimum(t, nu[0] - 1), jnp.where(t < nu[0], f, nf - 1)

    def wg_map(t, f, te, nu, src):
        tt, fe = eff(t, f, nu)
        return te[tt], 0, fe

    def wu_map(t, f, te, nu, src):
        tt, fe = eff(t, f, nu)
        return te[tt], 0, fe + nf

    def wd_map(t, f, te, nu, src):
        tt, fe = eff(t, f, nu)
        return te[tt], fe, 0

    def bd_map(t, f, te, nu, src):
        tt, _ = eff(t, f, nu)
        return te[tt], 0, 0

    def out_map(t, f, te, nu, src):
        return t, 0

    grid_spec = pltpu.PrefetchScalarGridSpec(
        num_scalar_prefetch=3,
        grid=(n_tiles, nf),
        in_specs=[pl.BlockSpec(memory_space=pl.ANY),
                  pl.BlockSpec((None, d, tf), wg_map), pl.BlockSpec((None, d, tf), wu_map),
                  pl.BlockSpec((None, 1, tf), wg_map), pl.BlockSpec((None, 1, tf), wu_map),
                  pl.BlockSpec((None, tf, d), wd_map), pl.BlockSpec((None, 1, d), bd_map)],
        out_specs=pl.BlockSpec((tb, d), out_map),
        scratch_shapes=[pltpu.VMEM((2, tb, d), F32), pltpu.VMEM((tb, d), BF16), pltpu.SemaphoreType.DMA((2,))])
    return pl.pallas_call(
        _expert_kernel, grid_spec=grid_spec,
        out_shape=jax.ShapeDtypeStruct((n_tiles * tb, d), F32),
        compiler_params=_cparams("arbitrary", "arbitrary"),
        name="routed_experts",
    )(tile_expert, n_used, src_token, hn, w_up, w_up, b_up.reshape(n_exp, 1, ff2), b_up.reshape(n_exp, 1, ff2),
      w_down, b_down.reshape(n_exp, 1, d))


def _combine_kernel(dest_ref, ys_hbm, xp_ref, xs_ref, gate_ref, g2_ref, fn_ref, yp_ref, ysm_ref, buf, sem,
                    *, n_prompt_tiles):
    i = pl.program_id(0)
    n = pl.num_programs(0)
    tm = xp_ref.shape[0]
    slot = i % 2

    def gather(tile, into):
        base = tile * tm * TOP_K

        @pl.loop(0, tm)
        def _(r):
            for k in range(TOP_K):
                row = dest_ref[base + r * TOP_K + k]
                pltpu.make_async_copy(ys_hbm.at[pl.ds(row, 1)], buf.at[into, k, pl.ds(r, 1)], sem.at[into]).start()

    @pl.when(i == 0)
    def _():
        gather(0, 0)

    @pl.when(i + 1 < n)
    def _():
        gather(i + 1, 1 - slot)

    for k in range(TOP_K):
        pltpu.make_async_copy(ys_hbm.at[pl.ds(0, tm)], buf.at[slot, k], sem.at[slot]).wait()

    moe = gate_ref[:, 0:1] * buf[slot, 0]
    for k in range(1, TOP_K):
        moe = moe + gate_ref[:, k:k + 1] * buf[slot, k]
    x = jnp.where(i < n_prompt_tiles, xp_ref[...], xs_ref[...])
    y = _rms(x + g2_ref[...] * moe, fn_ref[...])

    @pl.when(i < n_prompt_tiles)
    def _():
        yp_ref[...] = y

    @pl.when(i >= n_prompt_tiles)
    def _():
        ysm_ref[...] = y


def _combine(dest_flat, ys, x1p, x1s, gate, g2, final_norm, tiles_per_group):
    tm, d = x1s.shape
    n_prompt_rows = x1p.shape[0]
    r = n_prompt_rows + tm
    n_groups = g2.shape[0]
    npt = n_prompt_rows // tm
    grid_spec = pltpu.PrefetchScalarGridSpec(
        num_scalar_prefetch=1,
        grid=(r // tm,),
        in_specs=[pl.BlockSpec(memory_space=pl.ANY),
                  pl.BlockSpec((tm, d), lambda i, dst: (jnp.minimum(i, npt - 1), 0)),
                  pl.BlockSpec((tm, d), lambda i, dst: (0, 0)),
                  pl.BlockSpec((tm, TOP_K), lambda i, dst: (i, 0)),
                  pl.BlockSpec((None, tm, d), lambda i, dst: (jnp.minimum(i // tiles_per_group, n_groups - 1), 0, 0)),
                  pl.BlockSpec((1, d), lambda i, dst: (0, 0))],
        out_specs=[pl.BlockSpec((tm, d), lambda i, dst: (jnp.minimum(i, npt - 1), 0)),
                   pl.BlockSpec((tm, d), lambda i, dst: (0, 0))],
        scratch_shapes=[pltpu.VMEM((2, TOP_K, tm, d), F32), pltpu.SemaphoreType.DMA((2,))])
    return pl.pallas_call(
        functools.partial(_combine_kernel, n_prompt_tiles=npt), grid_spec=grid_spec,
        out_shape=[jax.ShapeDtypeStruct((n_prompt_rows, d), F32), jax.ShapeDtypeStruct((tm, d), F32)],
        compiler_params=_cparams("arbitrary"),
        name="expert_combine_final_norm",
    )(dest_flat, ys, x1p, x1s, gate, g2, final_norm.reshape(1, d))


def kernel(x_prompt, x_sample, c_prompt, c_sample, cache_k, cache_v, state_conv, state_h, page_table, rel_bias, w_ada, b_ada, norm_mix, norm_ffn, w_in, lam_q1, lam_k1, lam_q2, lam_k2, subln_gain, w_conv, b_conv, w_rg, b_rg, w_ig, b_ig, lam_rg, w_att_out, w_rnn_out, w_o, w_router, b_router, w_up, b_up, w_down, b_down, final_norm):
    bp, seq, d = x_prompt.shape
    bs, t_new, _ = x_sample.shape
    depth, n_pool, page, n_heads, dv = cache_v.shape
    assert page == PAGE_SIZE and cache_k.shape[-1] == dv
    d_rnn = w_conv.shape[-1]
    qk_w = n_heads * dv
    n_exp = w_router.shape[-1]
    rp, rs = bp * seq, bs * t_new
    scale = (dv // 2) ** -0.5
    assert depth == 1
    tm_p, tm_s = PROMPT_ROW_TILE, rs
    tok_tile = rs
    assert seq % tm_p == 0 and rp % tok_tile == 0 and rs % SUBLANES == 0
    expert_rows, ff_tile, tn = EXPERT_ROW_TILE, FF_TILE, WEIGHT_COL_TILE

    xp = x_prompt.reshape(rp, d)
    xs = x_sample.reshape(rs, d)
    c_all = jnp.concatenate([c_prompt, c_sample, jnp.zeros((-(bp + bs) % SUBLANES, d), F32)], axis=0)
    cache_k2 = cache_k.reshape(depth * n_pool, page * n_heads, dv)
    cache_v2 = cache_v.reshape(depth * n_pool, page * n_heads, dv)
    rel_bias_t = rel_bias.T
    rbt = jnp.tile(rel_bias, (1, page))

    outs = {k: [] for k in ("kp", "vp", "cp", "hp", "ks", "vs", "cs", "hs")}
    for l in range(depth):
        lam_init = 0.8 - 0.6 * math.exp(-0.3 * l)
        lam_vecs = jnp.stack([lam_q1[l], lam_k1[l], lam_q2[l], lam_k2[l]])
        mod = _ada(c_all, w_ada[l], b_ada[l])
        mods_p = [m.reshape(bp, 1, d) for m in jnp.split(mod[:bp], 6, axis=-1)]
        mods_s = [jnp.repeat(m, t_new, axis=0).reshape(1, rs, d) for m in jnp.split(mod[bp:bp + bs], 6, axis=-1)]

        x1 = []
        for grp, (x, mods, tm, tpg) in enumerate(((xp, mods_p, tm_p, seq // tm_p), (xs, mods_s, tm_s, 1))):
            sh1, sc1, g1 = mods[0], mods[1], mods[2]
            hn = _prenorm(x, norm_mix[l], sc1, sh1, tm, tpg)
            (q,) = _proj(hn, w_in[l], 0, qk_w, [BF16], tm, tn, scale=scale, name="proj_q")
            k32, kbf = _proj(hn, w_in[l], qk_w, qk_w, [F32, BF16], tm, tn, name="proj_k")
            v32, vbf = _proj(hn, w_in[l], 2 * qk_w, qk_w, [F32, BF16], tm, tn, name="proj_v")
            (rg,) = _proj(hn, w_in[l], 3 * qk_w, 2 * d_rnn, [F32], tm, tn, name="proj_rnn")
            (gates,) = _proj(hn, w_in[l], 3 * qk_w + 2 * d_rnn, 2 * d, [F32], tm, tn, name="proj_gates")

            if grp == 0:
                o_att = _prompt_attention(q, kbf, vbf, rel_bias_t, lam_vecs, subln_gain[l], bp, seq, lam_init)
                hr, h_last = _rglru_prompt(rg, w_conv[l], b_conv[l], w_rg[l], b_rg[l], w_ig[l], b_ig[l], lam_rg[l],
                                           bp, seq)
                xr3 = rg[:, :d_rnn].reshape(bp, seq, d_rnn)
                outs["kp"].append(k32.reshape(bp, seq, n_heads, dv))
                outs["vp"].append(v32.reshape(bp, seq, n_heads, dv))
                outs["cp"].append(xr3[:, seq - (w_conv.shape[1] - 1):])
                outs["hp"].append(h_last.reshape(bp, d_rnn))
            else:
                dk = dv // 2
                q5 = q.reshape(bs, t_new, n_heads, 2, dk).transpose(0, 3, 2, 1, 4)
                zeros = jnp.zeros_like(q5[:, 0])
                qb = jnp.stack([jnp.concatenate([q5[:, 0], zeros], axis=-1),
                                jnp.concatenate([zeros, q5[:, 1]], axis=-1)], axis=1)
                qb = qb.reshape(bs, 2 * n_heads * t_new, dv)
                pad = ((0, 0), (0, PAGE_SIZE - t_new * n_heads), (0, 0))
                k_self = jnp.pad(kbf.reshape(bs, t_new * n_heads, dv), pad)
                v_self = jnp.pad(vbf.reshape(bs, t_new * n_heads, dv), pad)
                o_s = _sample_attention(qb, k_self, v_self, cache_k2, cache_v2, page_table, l * n_pool, rbt,
                                        lam_vecs, subln_gain[l], lam_init, n_heads, t_new)
                o_att = o_s.reshape(bs, n_heads, t_new, dv).transpose(0, 2, 1, 3).reshape(rs, n_heads * dv)
                x_t = rg[:, :d_rnn].reshape(bs, t_new, d_rnn).transpose(1, 0, 2)
                g_t = rg[:, d_rnn:].reshape(bs, t_new, d_rnn).transpose(1, 0, 2)
                conv_t = state_conv[l].transpose(1, 0, 2)
                hr_t, h_last = _rglru_sample(x_t, g_t, conv_t, state_h[l], w_conv[l], b_conv[l], w_rg[l], b_rg[l],
                                             w_ig[l], b_ig[l], lam_rg[l])
                hr = hr_t.transpose(1, 0, 2).reshape(rs, d_rnn)
                buf = jnp.concatenate([state_conv[l], rg[:, :d_rnn].reshape(bs, t_new, d_rnn)], axis=1)
                outs["ks"].append(k32.reshape(bs, t_new, n_heads, dv))
                outs["vs"].append(v32.reshape(bs, t_new, n_heads, dv))
                outs["cs"].append(buf[:, t_new:])
                outs["hs"].append(h_last)

            mixed = _mix(o_att, hr, w_att_out[l], w_rnn_out[l], gates, tm, tn)
            x1.append(_resid(mixed, w_o[l], x, g1, tm, tn, tpg))

        n_tok = rp + rs
        tpg2 = seq // tok_tile

        def per_tile(mp, ms):
            return jnp.concatenate([jnp.broadcast_to(mp, (bp, tok_tile, d)), ms], axis=0)

        sh2, sc2, g2 = (per_tile(mods_p[j], mods_s[j]) for j in (3, 4, 5))
        hn2, top_idx, gate, rank, counts = _router(x1[0], x1[1], norm_ffn[l], sc2, sh2, w_router[l], b_router[l],
                                                   tpg2)
        counts = counts[0]
        padded = (counts + expert_rows - 1) // expert_rows * expert_rows
        pad_end = jnp.cumsum(padded)
        dest = ((pad_end - padded)[top_idx] + rank).reshape(-1)
        n_tiles = (n_tok * TOP_K) // expert_rows + n_exp
        n_used = (pad_end[-1] // expert_rows).astype(I32).reshape(1)
        tile_expert = jnp.minimum(
            jnp.searchsorted(pad_end, jnp.arange(n_tiles, dtype=I32) * expert_rows, side="right"), n_exp - 1
        ).astype(I32)
        src_token = jnp.zeros((n_tiles * expert_rows,), I32).at[dest].set(
            jnp.arange(n_tok * TOP_K, dtype=I32) // TOP_K)
        ys = _experts(hn2, tile_expert, n_used, src_token, w_up[l], b_up[l], w_down[l], b_down[l],
                      expert_rows, ff_tile)
        yp, ysm = _combine(dest.astype(I32), ys, x1[0], x1[1], gate, g2, final_norm, tpg2)
        xp, xs = yp, ysm

    return (xp.reshape(bp, seq, d), xs.reshape(bs, t_new, d),
            jnp.stack(outs["kp"]), jnp.stack(outs["vp"]), jnp.stack(outs["cp"]), jnp.stack(outs["hp"]),
            jnp.stack(outs["ks"]), jnp.stack(outs["vs"]), jnp.stack(outs["cs"]), jnp.stack(outs["hs"]))
```

```python
import functools
import math

import jax
import jax.numpy as jnp
from jax import lax
from jax.experimental import pallas as pl
from jax.experimental.pallas import tpu as pltpu

F32, BF16, I32 = jnp.float32, jnp.bfloat16, jnp.int32

TOP_K = 4
RG_C = 8.0
SWIGLU_LIMIT = 7.0
SWIGLU_ALPHA = 1.702
NORM_EPS = 1e-6
MAX_DISTANCE = 128
PAGE_SIZE = 128
MASKED = -1e30

V7X_VMEM_LIMIT_BYTES = 56 * 1024 * 1024
SUBLANES = 8

PROMPT_ROW_TILE = 512
WEIGHT_COL_TILE = 1024
EXPERT_ROW_TILE = 1024
EXPERT_SUB_ROWS = 256
GATHER_CHUNK = 64
FF_TILE = 256
ATTN_BLOCK = 256
ATTN_FAR_GROUP = 3

NT_DIMS = (((1,), (1,)), ((), ()))


def _cparams(*sem):
    return pltpu.CompilerParams(dimension_semantics=sem, vmem_limit_bytes=V7X_VMEM_LIMIT_BYTES)


def _rms(x, gain):
    return x * lax.rsqrt(jnp.mean(x * x, axis=-1, keepdims=True) + NORM_EPS) * gain


def _gelu_tanh(x):
    return 0.5 * x * (1.0 + jnp.tanh(math.sqrt(2.0 / math.pi) * (x + 0.044715 * (x * x * x))))


def _softplus(x):
    return jnp.maximum(x, 0.0) + jnp.log1p(jnp.exp(-jnp.abs(x)))


def _t5_bucket(dist, n_buckets):
    n = jnp.maximum(dist, 0)
    max_exact = n_buckets // 2
    nf = jnp.maximum(n, 1).astype(F32)
    large = max_exact + (jnp.log(nf / max_exact) / math.log(MAX_DISTANCE / max_exact)
                         * (n_buckets - max_exact)).astype(I32)
    large = jnp.minimum(large, n_buckets - 1)
    return jnp.where(n < max_exact, n, large)


def _diff_lambda(lam_ref, lam_init):
    lq = lam_ref[...]
    s1 = jnp.sum(lq[0:1] * lq[1:2], axis=-1, keepdims=True)
    s2 = jnp.sum(lq[2:3] * lq[3:4], axis=-1, keepdims=True)
    return jnp.exp(s1) - jnp.exp(s2) + lam_init


def _ada_kernel(c_ref, w_ref, b_ref, o_ref):
    c = c_ref[...]
    s = (c * jax.nn.sigmoid(c)).astype(BF16)
    o_ref[...] = jnp.dot(s, w_ref[...].astype(BF16), preferred_element_type=F32) + b_ref[...]


def _ada(c, w, b, tn=WEIGHT_COL_TILE):
    m, d = c.shape
    n = w.shape[1]
    return pl.pallas_call(
        _ada_kernel,
        grid=(n // tn,),
        in_specs=[pl.BlockSpec((m, d), lambda j: (0, 0)),
                  pl.BlockSpec((d, tn), lambda j: (0, j)),
                  pl.BlockSpec((1, tn), lambda j: (0, j))],
        out_specs=pl.BlockSpec((m, tn), lambda j: (0, j)),
        out_shape=jax.ShapeDtypeStruct((m, n), F32),
        compiler_params=_cparams("arbitrary"),
        name="adaln",
    )(c, w, b.reshape(1, n))


def _prenorm_kernel(x_ref, g_ref, sc_ref, sh_ref, o_ref):
    y = _rms(x_ref[...], g_ref[...])
    o_ref[...] = (y * (1.0 + sc_ref[...]) + sh_ref[...]).astype(o_ref.dtype)


def _prenorm(x, gain, sc, sh, tm, tiles_per_group):
    r, d = x.shape
    rm = sc.shape[1]
    mod_spec = pl.BlockSpec((None, rm, d), lambda i: (i // tiles_per_group, 0, 0))
    return pl.pallas_call(
        _prenorm_kernel,
        grid=(r // tm,),
        in_specs=[pl.BlockSpec((tm, d), lambda i: (i, 0)),
                  pl.BlockSpec((1, d), lambda i: (0, 0)), mod_spec, mod_spec],
        out_specs=pl.BlockSpec((tm, d), lambda i: (i, 0)),
        out_shape=jax.ShapeDtypeStruct((r, d), BF16),
        compiler_params=_cparams("arbitrary"),
        name="prenorm_mix",
    )(x, gain.reshape(1, d), sc, sh)


def _proj_kernel(a_ref, w_ref, *refs, scale):
    outs, wbf = refs[:-1], refs[-1]

    @pl.when(pl.program_id(1) == 0)
    def _():
        wbf[...] = w_ref[...].astype(BF16)

    acc = jnp.dot(a_ref[...], wbf[...], preferred_element_type=F32)
    if scale != 1.0:
        acc = acc * scale
    for o in outs:
        o[...] = acc.astype(o.dtype)


def _proj(a, w, col_off, n_cols, out_dtypes, tm, tn, scale=1.0, name="proj"):
    m, k = a.shape
    off = col_off // tn
    out_spec = pl.BlockSpec((tm, tn), lambda j, i: (i, j))
    outs = pl.pallas_call(
        functools.partial(_proj_kernel, scale=scale),
        grid=(n_cols // tn, m // tm),
        in_specs=[pl.BlockSpec((tm, k), lambda j, i: (i, 0)),
                  pl.BlockSpec((k, tn), lambda j, i: (0, j + off))],
        out_specs=[out_spec] * len(out_dtypes),
        out_shape=[jax.ShapeDtypeStruct((m, n_cols), dt) for dt in out_dtypes],
        scratch_shapes=[pltpu.VMEM((k, tn), BF16)],
        compiler_params=_cparams("arbitrary", "arbitrary"),
        name=name,
    )(a, w)
    return outs


def _mix_kernel(a1_ref, a2_ref, w1_ref, w2_ref, ga_ref, gb_ref, o_ref, w1bf, w2bf):
    @pl.when(pl.program_id(1) == 0)
    def _():
        w1bf[...] = w1_ref[...].astype(BF16)
        w2bf[...] = w2_ref[...].astype(BF16)

    y1 = jnp.dot(a1_ref[...], w1bf[...], preferred_element_type=F32)
    y2 = jnp.dot(a2_ref[...], w2bf[...], preferred_element_type=F32)
    o_ref[...] = (jax.nn.sigmoid(ga_ref[...]) * y1 + jax.nn.sigmoid(gb_ref[...]) * y2).astype(o_ref.dtype)


def _mix(a1, a2, w1, w2, gates, tm, tn):
    m, k = a1.shape
    n = w1.shape[1]
    nb = n // tn
    return pl.pallas_call(
        _mix_kernel,
        grid=(nb, m // tm),
        in_specs=[pl.BlockSpec((tm, k), lambda j, i: (i, 0)),
                  pl.BlockSpec((tm, k), lambda j, i: (i, 0)),
                  pl.BlockSpec((k, tn), lambda j, i: (0, j)),
                  pl.BlockSpec((k, tn), lambda j, i: (0, j)),
                  pl.BlockSpec((tm, tn), lambda j, i: (i, j)),
                  pl.BlockSpec((tm, tn), lambda j, i: (i, j + nb))],
        out_specs=pl.BlockSpec((tm, tn), lambda j, i: (i, j)),
        out_shape=jax.ShapeDtypeStruct((m, n), BF16),
        scratch_shapes=[pltpu.VMEM((k, tn), BF16), pltpu.VMEM((k, tn), BF16)],
        compiler_params=_cparams("arbitrary", "arbitrary"),
        name="branch_mix",
    )(a1, a2, w1, w2, gates, gates)


def _resid_kernel(a_ref, w_ref, x_ref, g_ref, o_ref, wbf):
    @pl.when(pl.program_id(1) == 0)
    def _():
        wbf[...] = w_ref[...].astype(BF16)

    y = jnp.dot(a_ref[...], wbf[...], preferred_element_type=F32)
    o_ref[...] = x_ref[...] + g_ref[...] * y


def _resid(a, w, x, g, tm, tn, tiles_per_group):
    m, k = a.shape
    n = w.shape[1]
    rm = g.shape[1]
    return pl.pallas_call(
        _resid_kernel,
        grid=(n // tn, m // tm),
        in_specs=[pl.BlockSpec((tm, k), lambda j, i: (i, 0)),
                  pl.BlockSpec((k, tn), lambda j, i: (0, j)),
                  pl.BlockSpec((tm, tn), lambda j, i: (i, j)),
                  pl.BlockSpec((None, rm, tn), lambda j, i: (i // tiles_per_group, 0, j))],
        out_specs=pl.BlockSpec((tm, tn), lambda j, i: (i, j)),
        out_shape=jax.ShapeDtypeStruct((m, n), F32),
        scratch_shapes=[pltpu.VMEM((k, tn), BF16)],
        compiler_params=_cparams("arbitrary", "arbitrary"),
        name="out_proj_residual",
    )(a, w, x, g)


def _pattn_kernel(rb_ref, lam_ref, gain_ref, q_ref, k_ref, vt_ref, o_ref, bias_s, m_s, l_s, acc_s,
                  *, blk, lam_init, n_buckets, far_group):
    h = pl.program_id(1)
    qi = pl.program_id(2)
    dk = q_ref.shape[1] // 2

    @pl.when(qi == 0)
    def _():
        r = lax.broadcasted_iota(I32, (blk, blk), 0)
        c = lax.broadcasted_iota(I32, (blk, blk), 1)
        far = rb_ref[h, n_buckets - 1]
        for which in range(2):
            dist = c - r + which * blk
            bkt = _t5_bucket(dist, n_buckets)
            val = jnp.zeros((blk, blk), F32)
            for b in range(n_buckets):
                val = jnp.where(bkt == b, rb_ref[h, b], val)
            val = val - far
            if which == 0:
                val = jnp.where(dist >= 0, val, MASKED)
            bias_s[which] = val

    q = q_ref[...]
    lane = lax.broadcasted_iota(I32, q.shape, 1)
    qmaps = (jnp.where(lane < dk, q, jnp.zeros_like(q)), jnp.where(lane >= dk, q, jnp.zeros_like(q)))

    m_s[...] = jnp.full(m_s.shape, -jnp.inf, F32)
    l_s[...] = jnp.zeros(l_s.shape, F32)
    acc_s[...] = jnp.zeros(acc_s.shape, F32)

    def process(kis, biases):
        ks = [k_ref[pl.ds(pl.multiple_of(ki * blk, blk), blk), :] for ki in kis]
        vts = [vt_ref[ki] for ki in kis]
        for mp in range(2):
            scores = []
            for k, bias in zip(ks, biases):
                s = lax.dot_general(k, qmaps[mp], NT_DIMS, preferred_element_type=F32)
                scores.append(s if bias is None else s + bias)
            smax = scores[0]
            for s in scores[1:]:
                smax = jnp.maximum(smax, s)
            m_old = m_s[mp]
            m_new = jnp.maximum(m_old, jnp.max(smax, axis=0, keepdims=True))
            corr = jnp.exp(m_old - m_new)
            probs = [jnp.exp(s - m_new) for s in scores]
            psum = probs[0]
            for p in probs[1:]:
                psum = psum + p
            pv = None
            for p, vt in zip(probs, vts):
                d = jnp.dot(vt, p.astype(BF16), preferred_element_type=F32)
                pv = d if pv is None else pv + d
            l_s[mp] = l_s[mp] * corr + jnp.sum(psum, axis=0, keepdims=True)
            acc_s[mp] = acc_s[mp] * corr + pv
            m_s[mp] = m_new

    n_far = jnp.maximum(qi - 1, 0)

    @pl.loop(0, n_far // far_group)
    def _(i):
        process([i * far_group + j for j in range(far_group)], [None] * far_group)

    @pl.loop(n_far // far_group * far_group, n_far)
    def _(ki):
        process([ki], [None])

    @pl.when(qi >= 1)
    def _():
        process([qi - 1, qi], [bias_s[1], bias_s[0]])

    @pl.when(qi == 0)
    def _():
        process([qi], [bias_s[0]])

    lam = _diff_lambda(lam_ref, lam_init)
    ot = acc_s[0] / l_s[0] - lam * (acc_s[1] / l_s[1])
    o = ot.T
    o_ref[...] = (_rms(o, gain_ref[...]) * (1.0 - lam_init)).astype(o_ref.dtype)


def _prompt_attention(q, k, vt, rel_bias_t, lam_vecs, gain, batch, seq, lam_init, blk):
    n_heads, n_buckets = rel_bias_t.shape
    dv = vt.shape[2]
    assert blk >= MAX_DISTANCE and seq % blk == 0
    nq = seq // blk
    return pl.pallas_call(
        functools.partial(_pattn_kernel, blk=blk, lam_init=lam_init, n_buckets=n_buckets,
                          far_group=ATTN_FAR_GROUP),
        grid=(batch, n_heads, nq),
        in_specs=[pl.BlockSpec(memory_space=pltpu.SMEM),
                  pl.BlockSpec(lam_vecs.shape, lambda b, h, i: (0, 0)),
                  pl.BlockSpec((1, dv), lambda b, h, i: (0, 0)),
                  pl.BlockSpec((blk, dv), lambda b, h, i: (b * nq + i, h)),
                  pl.BlockSpec((seq, dv), lambda b, h, i: (b, h)),
                  pl.BlockSpec((None, nq, dv, blk), lambda b, h, i: (b * n_heads + h, 0, 0, 0))],
        out_specs=pl.BlockSpec((blk, dv), lambda b, h, i: (b * nq + i, h)),
        out_shape=jax.ShapeDtypeStruct(q.shape, BF16),
        scratch_shapes=[pltpu.VMEM((2, blk, blk), F32), pltpu.VMEM((2, 1, blk), F32),
                        pltpu.VMEM((2, 1, blk), F32), pltpu.VMEM((2, dv, blk), F32)],
        compiler_params=_cparams("arbitrary", "arbitrary", "arbitrary"),
        name="prompt_diff_attention",
    )(rel_bias_t, lam_vecs, gain.reshape(1, dv), q, k, vt)


def _sattn_kernel(pt_ref, rbt_ref, lam_ref, gain_ref, qb_ref, ks_ref, vs_ref, ck_hbm, cv_hbm, o_ref,
                  kbuf, vbuf, mb_s, mbs_s, m_s, l_s, acc_s, sem,
                  *, group, n_pages, n_batch, page_off, lam_init, n_buckets, n_heads, t_new):
    b = pl.program_id(0)
    n_groups = n_pages // group
    rows = qb_ref.shape[0]
    half = rows // 2
    cols = kbuf.shape[2]
    scols = ks_ref.shape[0]

    @pl.when(b == 0)
    def _():
        def lookup(dist, width):
            bkt = _t5_bucket(dist, n_buckets)
            val = jnp.zeros(dist.shape, F32)
            for i in range(n_buckets):
                val = jnp.where(bkt == i, rbt_ref[i:i + 1, 0:width], val)
            return val - rbt_ref[n_buckets - 1:n_buckets, 0:width]

        r = lax.broadcasted_iota(I32, (rows, cols), 0)
        c = lax.broadcasted_iota(I32, (rows, cols), 1)
        hrow = (r % half) // t_new
        trow = r % t_new
        valid = hrow == c % n_heads
        mb_s[0] = jnp.where(valid, 0.0, MASKED)
        mb_s[1] = jnp.where(valid, lookup(PAGE_SIZE + trow - c // n_heads, cols), MASKED)
        r2 = lax.broadcasted_iota(I32, (rows, scols), 0)
        c2 = lax.broadcasted_iota(I32, (rows, scols), 1)
        t2 = r2 % t_new
        key2 = c2 // n_heads
        valid2 = ((r2 % half) // t_new == c2 % n_heads) & (key2 <= t2)
        mbs_s[...] = jnp.where(valid2, lookup(t2 - key2, scols), MASKED)

    def copies(bb, g, slot):
        out = []
        for j in range(group):
            page = pt_ref[bb, g * group + j] + page_off
            out.append(pltpu.make_async_copy(ck_hbm.at[page], kbuf.at[slot, j], sem.at[slot, 0, j]))
            out.append(pltpu.make_async_copy(cv_hbm.at[page], vbuf.at[slot, j], sem.at[slot, 1, j]))
        return out

    @pl.when(b == 0)
    def _():
        for cp in copies(0, 0, 0):
            cp.start()

    m_s[...] = jnp.full(m_s.shape, -jnp.inf, F32)
    l_s[...] = jnp.zeros(l_s.shape, F32)
    acc_s[...] = jnp.zeros(acc_s.shape, F32)
    qb = qb_ref[...]

    def update(blocks):
        scores = [lax.dot_general(qb, kp, NT_DIMS, preferred_element_type=F32) + mb for kp, _, mb in blocks]
        smax = scores[0]
        for s in scores[1:]:
            smax = jnp.maximum(smax, s)
        m_old = m_s[...]
        m_new = jnp.maximum(m_old, jnp.max(smax, axis=-1, keepdims=True))
        corr = jnp.exp(m_old - m_new)
        probs = [jnp.exp(s - m_new) for s in scores]
        psum = probs[0]
        for p in probs[1:]:
            psum = psum + p
        pv = None
        for p, (_, vp, _) in zip(probs, blocks):
            d = jnp.dot(p.astype(BF16), vp, preferred_element_type=F32)
            pv = d if pv is None else pv + d
        l_s[...] = l_s[...] * corr + jnp.sum(psum, axis=-1, keepdims=True)
        acc_s[...] = acc_s[...] * corr + pv
        m_s[...] = m_new

    @pl.loop(0, n_groups)
    def _(g):
        slot = g % 2
        nxt = g + 1

        @pl.when(nxt < n_groups)
        def _():
            for cp in copies(b, nxt, 1 - slot):
                cp.start()

        @pl.when((nxt == n_groups) & (b + 1 < n_batch))
        def _():
            for cp in copies(b + 1, 0, 1 - slot):
                cp.start()

        for cp in copies(b, g, slot):
            cp.wait()
        last = mb_s[jnp.where(g == n_groups - 1, 1, 0)]
        update([(kbuf[slot, j].astype(BF16), vbuf[slot, j].astype(BF16), last if j == group - 1 else mb_s[0])
                for j in range(group)])

    update([(ks_ref[...], vs_ref[...], mbs_s[...])])

    lam = _diff_lambda(lam_ref, lam_init)
    o = acc_s[...] / l_s[...]
    od = o[0:half] - lam * o[half:rows]
    o_ref[...] = (_rms(od, gain_ref[...]) * (1.0 - lam_init)).astype(o_ref.dtype)


def _sample_attention(qb, ks, vs, cache_k2, cache_v2, page_table, page_off, rbt, lam_vecs, gain, lam_init,
                      n_heads, t_new, group=8):
    n_batch, rows, dv = qb.shape
    n_pages = page_table.shape[1]
    cols = cache_k2.shape[1]
    n_buckets = rbt.shape[0]
    assert n_pages % group == 0 and (n_pages // group) % 2 == 0
    kern = functools.partial(_sattn_kernel, group=group, n_pages=n_pages, n_batch=n_batch, page_off=page_off,
                             lam_init=lam_init, n_buckets=n_buckets, n_heads=n_heads, t_new=t_new)
    grid_spec = pltpu.PrefetchScalarGridSpec(
        num_scalar_prefetch=1,
        grid=(n_batch,),
        in_specs=[pl.BlockSpec(rbt.shape, lambda b, pt: (0, 0)),
                  pl.BlockSpec(lam_vecs.shape, lambda b, pt: (0, 0)),
                  pl.BlockSpec((1, dv), lambda b, pt: (0, 0)),
                  pl.BlockSpec((None, rows, dv), lambda b, pt: (b, 0, 0)),
                  pl.BlockSpec((None,) + ks.shape[1:], lambda b, pt: (b, 0, 0)),
                  pl.BlockSpec((None,) + vs.shape[1:], lambda b, pt: (b, 0, 0)),
                  pl.BlockSpec(memory_space=pl.ANY),
                  pl.BlockSpec(memory_space=pl.ANY)],
        out_specs=pl.BlockSpec((None, rows // 2, dv), lambda b, pt: (b, 0, 0)),
        scratch_shapes=[pltpu.VMEM((2, group, cols, dv), F32), pltpu.VMEM((2, group, cols, dv), F32),
                        pltpu.VMEM((2, rows, cols), F32), pltpu.VMEM((rows, ks.shape[1]), F32),
                        pltpu.VMEM((rows, 1), F32), pltpu.VMEM((rows, 1), F32), pltpu.VMEM((rows, dv), F32),
                        pltpu.SemaphoreType.DMA((2, 2, group))])
    return pl.pallas_call(
        kern, grid_spec=grid_spec,
        out_shape=jax.ShapeDtypeStruct((n_batch, rows // 2, dv), BF16),
        compiler_params=_cparams("arbitrary"),
        name="sample_diff_attention",
    )(page_table, rbt, lam_vecs, gain.reshape(1, dv), qb, ks, vs, cache_k2, cache_v2)


def _rg_gates(xc, wrg_ref, brg_ref, wig_ref, big_ref, lam_ref):
    n_blocks, blk = wrg_ref.shape[0], wrg_ref.shape[1]
    rs, gs = [], []
    for n in range(n_blocks):
        xb = xc[:, n * blk:(n + 1) * blk].astype(BF16)
        rs.append(jnp.dot(xb, wrg_ref[n].astype(BF16), preferred_element_type=F32))
        gs.append(jnp.dot(xb, wig_ref[n].astype(BF16), preferred_element_type=F32))
    r = jax.nn.sigmoid(jnp.concatenate(rs, axis=-1) + brg_ref[...])
    i = jax.nn.sigmoid(jnp.concatenate(gs, axis=-1) + big_ref[...])
    log_a = (-RG_C * _softplus(-lam_ref[...])) * r
    a = jnp.exp(log_a)
    th = jnp.tanh(log_a)
    u = xc * i * jnp.sqrt(-2.0 * th / (1.0 - th))
    return a, u


def _rglru_prompt_kernel(x_ref, g_ref, wc_ref, bc_ref, wrg_ref, brg_ref, wig_ref, big_ref, lam_ref,
                         hr_ref, hl_ref, xbuf, a_s, u_s, h_s, *, conv_w):
    i = pl.program_id(1)
    tc = x_ref.shape[0]
    d = x_ref.shape[1]

    @pl.when(i == 0)
    def _():
        xbuf[0:SUBLANES] = jnp.zeros((SUBLANES, d), F32)
        h_s[...] = jnp.zeros(h_s.shape, F32)

    xbuf[SUBLANES:SUBLANES + tc] = x_ref[...]
    xc = bc_ref[...]
    for j in range(conv_w):
        lo = SUBLANES - (conv_w - 1) + j
        xc = xc + wc_ref[j:j + 1] * xbuf[lo:lo + tc]
    xbuf[0:SUBLANES] = xbuf[tc:tc + SUBLANES]

    a, u = _rg_gates(xc, wrg_ref, brg_ref, wig_ref, big_ref, lam_ref)
    a_s[...] = a
    u_s[...] = u
    row = lax.broadcasted_iota(I32, (SUBLANES, d), 0)

    @pl.loop(0, tc // SUBLANES)
    def _(g):
        st = pl.multiple_of(g * SUBLANES, SUBLANES)
        a8 = a_s[pl.ds(st, SUBLANES), :]
        u8 = u_s[pl.ds(st, SUBLANES), :]
        for s in (1, 2, 4):
            a_prev = jnp.where(row >= s, pltpu.roll(a8, s, 0), 1.0)
            u_prev = jnp.where(row >= s, pltpu.roll(u8, s, 0), 0.0)
            u8 = a8 * u_prev + u8
            a8 = a8 * a_prev
        h8 = a8 * h_s[...] + u8
        h_s[...] = jnp.broadcast_to(h8[SUBLANES - 1:SUBLANES], (SUBLANES, d))
        u_s[pl.ds(st, SUBLANES), :] = h8

    hr_ref[...] = (u_s[...] * _gelu_tanh(g_ref[...])).astype(hr_ref.dtype)

    @pl.when(i == pl.num_programs(1) - 1)
    def _():
        hl_ref[...] = h_s[0:1]


def _rglru_prompt(rg, w_conv, b_conv, w_rg, b_rg, w_ig, b_ig, lam_rg, batch, seq, tc=256):
    d = rg.shape[1] // 2
    nt = seq // tc
    conv_w = w_conv.shape[0]
    assert conv_w - 1 <= SUBLANES and seq % tc == 0
    vec = pl.BlockSpec((1, d), lambda b, i: (0, 0))
    wblk = pl.BlockSpec(w_rg.shape, lambda b, i: (0, 0, 0))
    return pl.pallas_call(
        functools.partial(_rglru_prompt_kernel, conv_w=conv_w),
        grid=(batch, nt),
        in_specs=[pl.BlockSpec((tc, d), lambda b, i: (b * nt + i, 0)),
                  pl.BlockSpec((tc, d), lambda b, i: (b * nt + i, 1)),
                  pl.BlockSpec((conv_w, d), lambda b, i: (0, 0)), vec, wblk, vec, wblk, vec, vec],
        out_specs=[pl.BlockSpec((tc, d), lambda b, i: (b * nt + i, 0)),
                   pl.BlockSpec((None, 1, d), lambda b, i: (b, 0, 0))],
        out_shape=[jax.ShapeDtypeStruct((batch * seq, d), BF16), jax.ShapeDtypeStruct((batch, 1, d), F32)],
        scratch_shapes=[pltpu.VMEM((tc + SUBLANES, d), F32), pltpu.VMEM((tc, d), F32),
                        pltpu.VMEM((tc, d), F32), pltpu.VMEM((SUBLANES, d), F32)],
        compiler_params=_cparams("arbitrary", "arbitrary"),
        name="rglru_prompt",
    )(rg, rg, w_conv, b_conv.reshape(1, d), w_rg, b_rg.reshape(1, d), w_ig, b_ig.reshape(1, d),
      lam_rg.reshape(1, d))


def _rglru_sample_kernel(x_ref, g_ref, cs_ref, h0_ref, wc_ref, bc_ref, wrg_ref, brg_ref, wig_ref, big_ref,
                         lam_ref, hr_ref, hl_ref, *, conv_w):
    t_new = x_ref.shape[0]
    buf = [cs_ref[j] for j in range(conv_w - 1)] + [x_ref[t] for t in range(t_new)]
    h = h0_ref[...]
    for t in range(t_new):
        xc = bc_ref[...]
        for j in range(conv_w):
            xc = xc + wc_ref[j:j + 1] * buf[t + j]
        a, u = _rg_gates(xc, wrg_ref, brg_ref, wig_ref, big_ref, lam_ref)
        h = a * h + u
        hr_ref[t] = (h * _gelu_tanh(g_ref[t])).astype(hr_ref.dtype)
    hl_ref[...] = h


def _rglru_sample(x_t, g_t, conv_t, h0, w_conv, b_conv, w_rg, b_rg, w_ig, b_ig, lam_rg):
    t_new, batch, d = x_t.shape
    return pl.pallas_call(
        functools.partial(_rglru_sample_kernel, conv_w=w_conv.shape[0]),
        out_shape=[jax.ShapeDtypeStruct((t_new, batch, d), BF16), jax.ShapeDtypeStruct((batch, d), F32)],
        compiler_params=pltpu.CompilerParams(vmem_limit_bytes=V7X_VMEM_LIMIT_BYTES),
        name="rglru_sample",
    )(x_t, g_t, conv_t, h0, w_conv, b_conv.reshape(1, d), w_rg, b_rg.reshape(1, d), w_ig, b_ig.reshape(1, d),
      lam_rg.reshape(1, d))


def _router_kernel(xp_ref, xs_ref, gain_ref, sc_ref, sh_ref, wr_ref, br_ref, hn_ref, idx_ref, gate_ref, rank_ref,
                   cnt_ref, base_s, *, n_prompt_tiles):
    i = pl.program_id(0)
    tm = xp_ref.shape[0]
    n_exp = wr_ref.shape[1]

    @pl.when(i == 0)
    def _():
        base_s[...] = jnp.zeros(base_s.shape, F32)

    x = jnp.where(i < n_prompt_tiles, xp_ref[...], xs_ref[...])
    hn = _rms(x, gain_ref[...]) * (1.0 + sc_ref[...]) + sh_ref[...]
    hn_ref[...] = hn
    logits = jnp.dot(hn.astype(BF16), wr_ref[...].astype(BF16), preferred_element_type=F32) + br_ref[...]

    lane = lax.broadcasted_iota(I32, (tm, n_exp), 1)
    work = logits
    sels, vals = [], []
    for _ in range(TOP_K):
        mx = jnp.max(work, axis=-1, keepdims=True)
        first = jnp.min(jnp.where(work == mx, lane, n_exp), axis=-1, keepdims=True)
        sel = lane == first
        sels.append(sel)
        vals.append(mx)
        idx_ref[:, len(sels) - 1:len(sels)] = first
        work = jnp.where(sel, -jnp.inf, work)

    es = [jnp.exp(v - vals[0]) for v in vals]
    tot = es[0]
    for e in es[1:]:
        tot = tot + e
    for k in range(TOP_K):
        gate_ref[:, k:k + 1] = es[k] / tot

    chosen = sels[0]
    for sel in sels[1:]:
        chosen = chosen | sel
    onehot = jnp.where(chosen, 1.0, 0.0)
    r = lax.broadcasted_iota(I32, (tm, tm), 0)
    c = lax.broadcasted_iota(I32, (tm, tm), 1)
    tri = jnp.where(r > c, 1.0, 0.0).astype(BF16)
    pos = jnp.dot(tri, onehot.astype(BF16), preferred_element_type=F32) + base_s[...]
    for k in range(TOP_K):
        rank_ref[:, k:k + 1] = jnp.sum(jnp.where(sels[k], pos, 0.0), axis=-1, keepdims=True).astype(I32)
    base_s[...] = base_s[...] + jnp.sum(onehot, axis=0, keepdims=True)
    cnt_ref[...] = base_s[...].astype(I32)


def _router(x1p, x1s, gain, sc, sh, w_router, b_router, tiles_per_group):
    tm, d = x1s.shape
    npt = x1p.shape[0] // tm
    r = x1p.shape[0] + tm
    n_exp = w_router.shape[1]
    n_groups = sc.shape[0]
    mod_spec = pl.BlockSpec((None, tm, d), lambda i: (jnp.minimum(i // tiles_per_group, n_groups - 1), 0, 0))
    small = pl.BlockSpec((tm, TOP_K), lambda i: (i, 0))
    return pl.pallas_call(
        functools.partial(_router_kernel, n_prompt_tiles=npt),
        grid=(r // tm,),
        in_specs=[pl.BlockSpec((tm, d), lambda i: (jnp.minimum(i, npt - 1), 0)),
                  pl.BlockSpec((tm, d), lambda i: (0, 0)), pl.BlockSpec((1, d), lambda i: (0, 0)),
                  mod_spec, mod_spec,
                  pl.BlockSpec((d, n_exp), lambda i: (0, 0)), pl.BlockSpec((1, n_exp), lambda i: (0, 0))],
        out_specs=[pl.BlockSpec((tm, d), lambda i: (i, 0)), small, small, small,
                   pl.BlockSpec((1, n_exp), lambda i: (0, 0))],
        out_shape=[jax.ShapeDtypeStruct((r, d), F32), jax.ShapeDtypeStruct((r, TOP_K), I32),
                   jax.ShapeDtypeStruct((r, TOP_K), F32), jax.ShapeDtypeStruct((r, TOP_K), I32),
                   jax.ShapeDtypeStruct((1, n_exp), I32)],
        scratch_shapes=[pltpu.VMEM((1, n_exp), F32)],
        compiler_params=_cparams("arbitrary"),
        name="router_topk",
    )(x1p, x1s, gain.reshape(1, d), sc, sh, w_router, b_router.reshape(1, n_exp))


def _expert_kernel(te_ref, rows_ref, nu_ref, src_ref, hn_hbm, wg_ref, wu_ref, bg_ref, bu_ref, wd_ref, bd_ref, o_ref,
                   xg, xb, wgb, wub, wdb, sem, *, sub, chunk):
    t = pl.program_id(0)
    f = pl.program_id(1)
    tb = xb.shape[0]
    n_used = nu_ref[0]
    n_sub = (rows_ref[t] + sub - 1) // sub

    def n_chunks(tile):
        return (rows_ref[tile] + chunk - 1) // chunk

    def gather(tile):
        base = tile * tb

        @pl.loop(0, n_chunks(tile) * chunk)
        def _(r):
            tok = src_ref[base + r]
            pltpu.make_async_copy(hn_hbm.at[pl.ds(tok, 1)], xg.at[pl.ds(r, 1)], sem.at[0]).start()

    @pl.when((t == 0) & (f == 0))
    def _():
        xb[...] = jnp.zeros(xb.shape, BF16)
        gather(0)

    @pl.when((t < n_used) & (f == 0))
    def _():
        @pl.loop(0, n_chunks(t))
        def _(c):
            pltpu.make_async_copy(hn_hbm.at[pl.ds(0, chunk)], xg.at[pl.ds(0, chunk)], sem.at[0]).wait()

        @pl.loop(0, n_chunks(t))
        def _(c):
            rs = pl.ds(pl.multiple_of(c * chunk, chunk), chunk)
            xb[rs, :] = xg[rs, :].astype(BF16)

        o_ref[...] = jnp.broadcast_to(bd_ref[...], o_ref.shape)

        @pl.when(t + 1 < n_used)
        def _():
            gather(t + 1)

    @pl.when(t < n_used)
    def _():
        wgb[...] = wg_ref[...].astype(BF16)
        wub[...] = wu_ref[...].astype(BF16)
        wdb[...] = wd_ref[...].astype(BF16)

        @pl.loop(0, n_sub)
        def _(sb):
            rs = pl.ds(pl.multiple_of(sb * sub, sub), sub)
            x = xb[rs, :]
            g = jnp.dot(x, wgb[...], preferred_element_type=F32) + bg_ref[...]
            u = jnp.dot(x, wub[...], preferred_element_type=F32) + bu_ref[...]
            g = jnp.minimum(g, SWIGLU_LIMIT)
            u = jnp.clip(u, -SWIGLU_LIMIT, SWIGLU_LIMIT)
            act = g * jax.nn.sigmoid(SWIGLU_ALPHA * g) * (u + 1.0)
            o_ref[rs, :] += jnp.dot(act.astype(BF16), wdb[...], preferred_element_type=F32)

    @pl.when((t >= n_used) & (f == 0))
    def _():
        o_ref[...] = jnp.zeros(o_ref.shape, F32)


def _experts(hn, tile_expert, tile_rows, n_used, src_token, w_up, b_up, w_down, b_down, tb, tf, sub, chunk):
    n_exp, d, ff2 = w_up.shape
    ff = ff2 // 2
    nf = ff // tf
    n_tiles = tile_expert.shape[0]
    assert tb % sub == 0 and sub % chunk == 0

    def eff(t, f, nu):
        return jnp.minimum(t, nu[0] - 1), jnp.where(t < nu[0], f, nf - 1)

    def wg_map(t, f, te, rows, nu, src):
        tt, fe = eff(t, f, nu)
        return te[tt], 0, fe

    def wu_map(t, f, te, rows, nu, src):
        tt, fe = eff(t, f, nu)
        return te[tt], 0, fe + nf

    def wd_map(t, f, te, rows, nu, src):
        tt, fe = eff(t, f, nu)
        return te[tt], fe, 0

    def bd_map(t, f, te, rows, nu, src):
        tt, _ = eff(t, f, nu)
        return te[tt], 0, 0

    def out_map(t, f, te, rows, nu, src):
        return t, 0

    grid_spec = pltpu.PrefetchScalarGridSpec(
        num_scalar_prefetch=4,
        grid=(n_tiles, nf),
        in_specs=[pl.BlockSpec(memory_space=pl.ANY),
                  pl.BlockSpec((None, d, tf), wg_map), pl.BlockSpec((None, d, tf), wu_map),
                  pl.BlockSpec((None, 1, tf), wg_map), pl.BlockSpec((None, 1, tf), wu_map),
                  pl.BlockSpec((None, tf, d), wd_map), pl.BlockSpec((None, 1, d), bd_map)],
        out_specs=pl.BlockSpec((tb, d), out_map),
        scratch_shapes=[pltpu.VMEM((tb, d), F32), pltpu.VMEM((tb, d), BF16),
                        pltpu.VMEM((d, tf), BF16), pltpu.VMEM((d, tf), BF16), pltpu.VMEM((tf, d), BF16),
                        pltpu.SemaphoreType.DMA((1,))])
    return pl.pallas_call(
        functools.partial(_expert_kernel, sub=sub, chunk=chunk), grid_spec=grid_spec,
        out_shape=jax.ShapeDtypeStruct((n_tiles * tb, d), F32),
        compiler_params=_cparams("arbitrary", "arbitrary"),
        name="routed_experts",
    )(tile_expert, tile_rows, n_used, src_token, hn, w_up, w_up, b_up.reshape(n_exp, 1, ff2),
      b_up.reshape(n_exp, 1, ff2), w_down, b_down.reshape(n_exp, 1, d))


def _combine_kernel(dest_ref, ys_hbm, xp_ref, xs_ref, gate_ref, g2_ref, fn_ref, yp_ref, ysm_ref, buf, sem,
                    *, n_prompt_tiles):
    i = pl.program_id(0)
    n = pl.num_programs(0)
    tm = xp_ref.shape[0]
    slot = i % 2

    def gather(tile, into):
        base = tile * tm * TOP_K

        @pl.loop(0, tm)
        def _(r):
            for k in range(TOP_K):
                row = dest_ref[base + r * TOP_K + k]
                pltpu.make_async_copy(ys_hbm.at[pl.ds(row, 1)], buf.at[into, k, pl.ds(r, 1)], sem.at[into]).start()

    @pl.when(i == 0)
    def _():
        gather(0, 0)

    @pl.when(i + 1 < n)
    def _():
        gather(i + 1, 1 - slot)

    for k in range(TOP_K):
        pltpu.make_async_copy(ys_hbm.at[pl.ds(0, tm)], buf.at[slot, k], sem.at[slot]).wait()

    moe = gate_ref[:, 0:1] * buf[slot, 0]
    for k in range(1, TOP_K):
        moe = moe + gate_ref[:, k:k + 1] * buf[slot, k]
    x = jnp.where(i < n_prompt_tiles, xp_ref[...], xs_ref[...])
    y = _rms(x + g2_ref[...] * moe, fn_ref[...])

    @pl.when(i < n_prompt_tiles)
    def _():
        yp_ref[...] = y

    @pl.when(i >= n_prompt_tiles)
    def _():
        ysm_ref[...] = y


def _combine(dest_flat, ys, x1p, x1s, gate, g2, final_norm, tiles_per_group):
    tm, d = x1s.shape
    n_prompt_rows = x1p.shape[0]
    r = n_prompt_rows + tm
    n_groups = g2.shape[0]
    npt = n_prompt_rows // tm
    grid_spec = pltpu.PrefetchScalarGridSpec(
        num_scalar_prefetch=1,
        grid=(r // tm,),
        in_specs=[pl.BlockSpec(memory_space=pl.ANY),
                  pl.BlockSpec((tm, d), lambda i, dst: (jnp.minimum(i, npt - 1), 0)),
                  pl.BlockSpec((tm, d), lambda i, dst: (0, 0)),
                  pl.BlockSpec((tm, TOP_K), lambda i, dst: (i, 0)),
                  pl.BlockSpec((None, tm, d), lambda i, dst: (jnp.minimum(i // tiles_per_group, n_groups - 1), 0, 0)),
                  pl.BlockSpec((1, d), lambda i, dst: (0, 0))],
        out_specs=[pl.BlockSpec((tm, d), lambda i, dst: (jnp.minimum(i, npt - 1), 0)),
                   pl.BlockSpec((tm, d), lambda i, dst: (0, 0))],
        scratch_shapes=[pltpu.VMEM((2, TOP_K, tm, d), F32), pltpu.SemaphoreType.DMA((2,))])
    return pl.pallas_call(
        functools.partial(_combine_kernel, n_prompt_tiles=npt), grid_spec=grid_spec,
        out_shape=[jax.ShapeDtypeStruct((n_prompt_rows, d), F32), jax.ShapeDtypeStruct((tm, d), F32)],
        compiler_params=_cparams("arbitrary"),
        name="expert_combine_final_norm",
    )(dest_flat, ys, x1p, x1s, gate, g2, final_norm.reshape(1, d))


def kernel(x_prompt, x_sample, c_prompt, c_sample, cache_k, cache_v, state_conv, state_h, page_table, rel_bias, w_ada, b_ada, norm_mix, norm_ffn, w_in, lam_q1, lam_k1, lam_q2, lam_k2, subln_gain, w_conv, b_conv, w_rg, b_rg, w_ig, b_ig, lam_rg, w_att_out, w_rnn_out, w_o, w_router, b_router, w_up, b_up, w_down, b_down, final_norm):
    bp, seq, d = x_prompt.shape
    bs, t_new, _ = x_sample.shape
    depth, n_pool, page, n_heads, dv = cache_v.shape
    assert page == PAGE_SIZE and cache_k.shape[-1] == dv
    d_rnn = w_conv.shape[-1]
    qk_w = n_heads * dv
    n_exp = w_router.shape[-1]
    rp, rs = bp * seq, bs * t_new
    scale = (dv // 2) ** -0.5
    assert depth == 1
    tm_p, tm_s = PROMPT_ROW_TILE, rs
    tok_tile = rs
    assert seq % tm_p == 0 and rp % tok_tile == 0 and rs % SUBLANES == 0
    expert_rows, ff_tile, tn = EXPERT_ROW_TILE, FF_TILE, WEIGHT_COL_TILE

    xp = x_prompt.reshape(rp, d)
    xs = x_sample.reshape(rs, d)
    c_all = jnp.concatenate([c_prompt, c_sample, jnp.zeros((-(bp + bs) % SUBLANES, d), F32)], axis=0)
    cache_k2 = cache_k.reshape(depth * n_pool, page * n_heads, dv)
    cache_v2 = cache_v.reshape(depth * n_pool, page * n_heads, dv)
    rel_bias_t = rel_bias.T
    rbt = jnp.tile(rel_bias, (1, page))

    outs = {k: [] for k in ("kp", "vp", "cp", "hp", "ks", "vs", "cs", "hs")}
    for l in range(depth):
        lam_init = 0.8 - 0.6 * math.exp(-0.3 * l)
        lam_vecs = jnp.stack([lam_q1[l], lam_k1[l], lam_q2[l], lam_k2[l]])
        mod = _ada(c_all, w_ada[l], b_ada[l])
        mods_p = [m.reshape(bp, 1, d) for m in jnp.split(mod[:bp], 6, axis=-1)]
        mods_s = [jnp.repeat(m, t_new, axis=0).reshape(1, rs, d) for m in jnp.split(mod[bp:bp + bs], 6, axis=-1)]

        x1 = []
        for grp, (x, mods, tm, tpg) in enumerate(((xp, mods_p, tm_p, seq // tm_p), (xs, mods_s, tm_s, 1))):
            sh1, sc1, g1 = mods[0], mods[1], mods[2]
            hn = _prenorm(x, norm_mix[l], sc1, sh1, tm, tpg)
            (q,) = _proj(hn, w_in[l], 0, qk_w, [BF16], tm, tn, scale=scale, name="proj_q")
            k32, kbf = _proj(hn, w_in[l], qk_w, qk_w, [F32, BF16], tm, tn, name="proj_k")
            v32, vbf = _proj(hn, w_in[l], 2 * qk_w, qk_w, [F32, BF16], tm, tn, name="proj_v")
            (rg,) = _proj(hn, w_in[l], 3 * qk_w, 2 * d_rnn, [F32], tm, tn, name="proj_rnn")
            (gates,) = _proj(hn, w_in[l], 3 * qk_w + 2 * d_rnn, 2 * d, [F32], tm, tn, name="proj_gates")

            if grp == 0:
                nkb = seq // ATTN_BLOCK
                vt = vbf.reshape(bp, nkb, ATTN_BLOCK, n_heads, dv).transpose(0, 3, 1, 4, 2)
                vt = vt.reshape(bp * n_heads, nkb, dv, ATTN_BLOCK)
                o_att = _prompt_attention(q, kbf, vt, rel_bias_t, lam_vecs, subln_gain[l], bp, seq, lam_init,
                                          ATTN_BLOCK)
                hr, h_last = _rglru_prompt(rg, w_conv[l], b_conv[l], w_rg[l], b_rg[l], w_ig[l], b_ig[l], lam_rg[l],
                                           bp, seq)
                xr3 = rg[:, :d_rnn].reshape(bp, seq, d_rnn)
                outs["kp"].append(k32.reshape(bp, seq, n_heads, dv))
                outs["vp"].append(v32.reshape(bp, seq, n_heads, dv))
                outs["cp"].append(xr3[:, seq - (w_conv.shape[1] - 1):])
                outs["hp"].append(h_last.reshape(bp, d_rnn))
            else:
                dk = dv // 2
                q5 = q.reshape(bs, t_new, n_heads, 2, dk).transpose(0, 3, 2, 1, 4)
                zeros = jnp.zeros_like(q5[:, 0])
                qb = jnp.stack([jnp.concatenate([q5[:, 0], zeros], axis=-1),
                                jnp.concatenate([zeros, q5[:, 1]], axis=-1)], axis=1)
                qb = qb.reshape(bs, 2 * n_heads * t_new, dv)
                pad = ((0, 0), (0, PAGE_SIZE - t_new * n_heads), (0, 0))
                k_self = jnp.pad(kbf.reshape(bs, t_new * n_heads, dv), pad)
                v_self = jnp.pad(vbf.reshape(bs, t_new * n_heads, dv), pad)
                o_s = _sample_attention(qb, k_self, v_self, cache_k2, cache_v2, page_table, l * n_pool, rbt,
                                        lam_vecs, subln_gain[l], lam_init, n_heads, t_new)
                o_att = o_s.reshape(bs, n_heads, t_new, dv).transpose(0, 2, 1, 3).reshape(rs, n_heads * dv)
                x_t = rg[:, :d_rnn].reshape(bs, t_new, d_rnn).transpose(1, 0, 2)
                g_t = rg[:, d_rnn:].reshape(bs, t_new, d_rnn).transpose(1, 0, 2)
                conv_t = state_conv[l].transpose(1, 0, 2)
                hr_t, h_last = _rglru_sample(x_t, g_t, conv_t, state_h[l], w_conv[l], b_conv[l], w_rg[l], b_rg[l],
                                             w_ig[l], b_ig[l], lam_rg[l])
                hr = hr_t.transpose(1, 0, 2).reshape(rs, d_rnn)
                buf = jnp.concatenate([state_conv[l], rg[:, :d_rnn].reshape(bs, t_new, d_rnn)], axis=1)
                outs["ks"].append(k32.reshape(bs, t_new, n_heads, dv))
                outs["vs"].append(v32.reshape(bs, t_new, n_heads, dv))
                outs["cs"].append(buf[:, t_new:])
                outs["hs"].append(h_last)

            mixed = _mix(o_att, hr, w_att_out[l], w_rnn_out[l], gates, tm, tn)
            x1.append(_resid(mixed, w_o[l], x, g1, tm, tn, tpg))

        n_tok = rp + rs
        tpg2 = seq // tok_tile

        def per_tile(mp, ms):
            return jnp.concatenate([jnp.broadcast_to(mp, (bp, tok_tile, d)), ms], axis=0)

        sh2, sc2, g2 = (per_tile(mods_p[j], mods_s[j]) for j in (3, 4, 5))
        hn2, top_idx, gate, rank, counts = _router(x1[0], x1[1], norm_ffn[l], sc2, sh2, w_router[l], b_router[l],
                                                   tpg2)
        counts = counts[0]
        padded = (counts + expert_rows - 1) // expert_rows * expert_rows
        pad_end = jnp.cumsum(padded)
        pad_start = pad_end - padded
        dest = (pad_start[top_idx] + rank).reshape(-1)
        n_tiles = (n_tok * TOP_K) // expert_rows + n_exp
        n_used = (pad_end[-1] // expert_rows).astype(I32).reshape(1)
        tile_first = jnp.arange(n_tiles, dtype=I32) * expert_rows
        tile_expert = jnp.minimum(jnp.searchsorted(pad_end, tile_first, side="right"), n_exp - 1).astype(I32)
        tile_rows = jnp.clip(counts[tile_expert] - (tile_first - pad_start[tile_expert]), 0, expert_rows).astype(I32)
        src_token = jnp.zeros((n_tiles * expert_rows,), I32).at[dest].set(
            jnp.arange(n_tok * TOP_K, dtype=I32) // TOP_K)
        ys = _experts(hn2, tile_expert, tile_rows, n_used, src_token, w_up[l], b_up[l], w_down[l], b_down[l],
                      expert_rows, ff_tile, EXPERT_SUB_ROWS, GATHER_CHUNK)
        yp, ysm = _combine(dest.astype(I32), ys, x1[0], x1[1], gate, g2, final_norm, tpg2)
        xp, xs = yp, ysm

    return (xp.reshape(bp, seq, d), xs.reshape(bs, t_new, d),
            jnp.stack(outs["kp"]), jnp.stack(outs["vp"]), jnp.stack(outs["cp"]), jnp.stack(outs["hp"]),
            jnp.stack(outs["ks"]), jnp.stack(outs["vs"]), jnp.stack(outs["cs"]), jnp.stack(outs["hs"]))
```

```python
import functools
import math

import jax
import jax.numpy as jnp
import numpy as np
from jax import lax
from jax.experimental import pallas as pl
from jax.experimental.pallas import tpu as pltpu

F32, BF16, I32, U32 = jnp.float32, jnp.bfloat16, jnp.int32, jnp.uint32
HIGH_HALF = np.uint32(0xFFFF0000)

TOP_K = 4
RG_C = 8.0
SWIGLU_LIMIT = 7.0
SWIGLU_ALPHA = 1.702
NORM_EPS = 1e-6
MAX_DISTANCE = 128
PAGE_SIZE = 128
MASKED = -1e30

V7X_VMEM_LIMIT_BYTES = 56 * 1024 * 1024
SUBLANES = 8

PROMPT_ROW_TILE = 512
WEIGHT_COL_TILE = 1024
EXPERT_ROW_TILE = 1280
EXPERT_BIG_ROWS = 1024
GATHER_CHUNK = 64
GATHER_UNROLL = 8
FF_TILE = 256
ATTN_BLOCK = 256

NT_DIMS = (((1,), (1,)), ((), ()))


def _cparams(*sem):
    return pltpu.CompilerParams(dimension_semantics=sem, vmem_limit_bytes=V7X_VMEM_LIMIT_BYTES)


def _rms(x, gain):
    return x * lax.rsqrt(jnp.mean(x * x, axis=-1, keepdims=True) + NORM_EPS) * gain


def _gelu_tanh(x):
    return 0.5 * x * (1.0 + jnp.tanh(math.sqrt(2.0 / math.pi) * (x + 0.044715 * (x * x * x))))


def _softplus(x):
    return jnp.maximum(x, 0.0) + jnp.log1p(jnp.exp(-jnp.abs(x)))


def _t5_bucket(dist, n_buckets):
    n = jnp.maximum(dist, 0)
    max_exact = n_buckets // 2
    nf = jnp.maximum(n, 1).astype(F32)
    large = max_exact + (jnp.log(nf / max_exact) / math.log(MAX_DISTANCE / max_exact)
                         * (n_buckets - max_exact)).astype(I32)
    large = jnp.minimum(large, n_buckets - 1)
    return jnp.where(n < max_exact, n, large)


def _diff_lambda(lam_ref, lam_init):
    lq = lam_ref[...]
    s1 = jnp.sum(lq[0:1] * lq[1:2], axis=-1, keepdims=True)
    s2 = jnp.sum(lq[2:3] * lq[3:4], axis=-1, keepdims=True)
    return jnp.exp(s1) - jnp.exp(s2) + lam_init


def _ada_kernel(c_ref, w_ref, b_ref, o_ref):
    c = c_ref[...]
    s = (c * jax.nn.sigmoid(c)).astype(BF16)
    o_ref[...] = jnp.dot(s, w_ref[...].astype(BF16), preferred_element_type=F32) + b_ref[...]


def _ada(c, w, b, tn=WEIGHT_COL_TILE):
    m, d = c.shape
    n = w.shape[1]
    return pl.pallas_call(
        _ada_kernel,
        grid=(n // tn,),
        in_specs=[pl.BlockSpec((m, d), lambda j: (0, 0)),
                  pl.BlockSpec((d, tn), lambda j: (0, j)),
                  pl.BlockSpec((1, tn), lambda j: (0, j))],
        out_specs=pl.BlockSpec((m, tn), lambda j: (0, j)),
        out_shape=jax.ShapeDtypeStruct((m, n), F32),
        compiler_params=_cparams("arbitrary"),
        name="adaln",
    )(c, w, b.reshape(1, n))


def _prenorm_kernel(x_ref, g_ref, sc_ref, sh_ref, o_ref):
    y = _rms(x_ref[...], g_ref[...])
    o_ref[...] = (y * (1.0 + sc_ref[...]) + sh_ref[...]).astype(o_ref.dtype)


def _prenorm(x, gain, sc, sh, tm, tiles_per_group):
    r, d = x.shape
    rm = sc.shape[1]
    mod_spec = pl.BlockSpec((None, rm, d), lambda i: (i // tiles_per_group, 0, 0))
    return pl.pallas_call(
        _prenorm_kernel,
        grid=(r // tm,),
        in_specs=[pl.BlockSpec((tm, d), lambda i: (i, 0)),
                  pl.BlockSpec((1, d), lambda i: (0, 0)), mod_spec, mod_spec],
        out_specs=pl.BlockSpec((tm, d), lambda i: (i, 0)),
        out_shape=jax.ShapeDtypeStruct((r, d), BF16),
        compiler_params=_cparams("arbitrary"),
        name="prenorm_mix",
    )(x, gain.reshape(1, d), sc, sh)


def _proj_kernel(a_ref, w_ref, *refs, scale):
    outs, wbf = refs[:-1], refs[-1]

    @pl.when(pl.program_id(1) == 0)
    def _():
        wbf[...] = w_ref[...].astype(BF16)

    acc = jnp.dot(a_ref[...], wbf[...], preferred_element_type=F32)
    if scale != 1.0:
        acc = acc * scale
    for o in outs:
        o[...] = acc.astype(o.dtype)


def _proj(a, w, col_off, n_cols, out_dtypes, tm, tn, scale=1.0, name="proj"):
    m, k = a.shape
    off = col_off // tn
    out_spec = pl.BlockSpec((tm, tn), lambda j, i: (i, j))
    outs = pl.pallas_call(
        functools.partial(_proj_kernel, scale=scale),
        grid=(n_cols // tn, m // tm),
        in_specs=[pl.BlockSpec((tm, k), lambda j, i: (i, 0)),
                  pl.BlockSpec((k, tn), lambda j, i: (0, j + off))],
        out_specs=[out_spec] * len(out_dtypes),
        out_shape=[jax.ShapeDtypeStruct((m, n_cols), dt) for dt in out_dtypes],
        scratch_shapes=[pltpu.VMEM((k, tn), BF16)],
        compiler_params=_cparams("arbitrary", "arbitrary"),
        name=name,
    )(a, w)
    return outs


def _mix_kernel(a1_ref, a2_ref, w1_ref, w2_ref, ga_ref, gb_ref, o_ref, w1bf, w2bf):
    @pl.when(pl.program_id(1) == 0)
    def _():
        w1bf[...] = w1_ref[...].astype(BF16)
        w2bf[...] = w2_ref[...].astype(BF16)

    y1 = jnp.dot(a1_ref[...], w1bf[...], preferred_element_type=F32)
    y2 = jnp.dot(a2_ref[...], w2bf[...], preferred_element_type=F32)
    o_ref[...] = (jax.nn.sigmoid(ga_ref[...]) * y1 + jax.nn.sigmoid(gb_ref[...]) * y2).astype(o_ref.dtype)


def _mix(a1, a2, w1, w2, gates, tm, tn):
    m, k = a1.shape
    n = w1.shape[1]
    nb = n // tn
    return pl.pallas_call(
        _mix_kernel,
        grid=(nb, m // tm),
        in_specs=[pl.BlockSpec((tm, k), lambda j, i: (i, 0)),
                  pl.BlockSpec((tm, k), lambda j, i: (i, 0)),
                  pl.BlockSpec((k, tn), lambda j, i: (0, j)),
                  pl.BlockSpec((k, tn), lambda j, i: (0, j)),
                  pl.BlockSpec((tm, tn), lambda j, i: (i, j)),
                  pl.BlockSpec((tm, tn), lambda j, i: (i, j + nb))],
        out_specs=pl.BlockSpec((tm, tn), lambda j, i: (i, j)),
        out_shape=jax.ShapeDtypeStruct((m, n), BF16),
        scratch_shapes=[pltpu.VMEM((k, tn), BF16), pltpu.VMEM((k, tn), BF16)],
        compiler_params=_cparams("arbitrary", "arbitrary"),
        name="branch_mix",
    )(a1, a2, w1, w2, gates, gates)


def _resid_kernel(a_ref, w_ref, x_ref, g_ref, o_ref, wbf):
    @pl.when(pl.program_id(1) == 0)
    def _():
        wbf[...] = w_ref[...].astype(BF16)

    y = jnp.dot(a_ref[...], wbf[...], preferred_element_type=F32)
    o_ref[...] = x_ref[...] + g_ref[...] * y


def _resid(a, w, x, g, tm, tn, tiles_per_group):
    m, k = a.shape
    n = w.shape[1]
    rm = g.shape[1]
    return pl.pallas_call(
        _resid_kernel,
        grid=(n // tn, m // tm),
        in_specs=[pl.BlockSpec((tm, k), lambda j, i: (i, 0)),
                  pl.BlockSpec((k, tn), lambda j, i: (0, j)),
                  pl.BlockSpec((tm, tn), lambda j, i: (i, j)),
                  pl.BlockSpec((None, rm, tn), lambda j, i: (i // tiles_per_group, 0, j))],
        out_specs=pl.BlockSpec((tm, tn), lambda j, i: (i, j)),
        out_shape=jax.ShapeDtypeStruct((m, n), F32),
        scratch_shapes=[pltpu.VMEM((k, tn), BF16)],
        compiler_params=_cparams("arbitrary", "arbitrary"),
        name="out_proj_residual",
    )(a, w, x, g)


def _pattn_kernel(rb_ref, lam_ref, gain_ref, q_ref, k_ref, vt_ref, o_ref, bias_s, *, blk, lam_init, n_buckets, nq):
    h = pl.program_id(1)
    qi = pl.program_id(2)
    dk = q_ref.shape[1] // 2

    @pl.when(qi == 0)
    def _():
        r = lax.broadcasted_iota(I32, (blk, blk), 0)
        c = lax.broadcasted_iota(I32, (blk, blk), 1)
        far = rb_ref[h, n_buckets - 1]
        for which in range(2):
            dist = c - r + which * blk
            bkt = _t5_bucket(dist, n_buckets)
            val = jnp.zeros((blk, blk), F32)
            for b in range(n_buckets):
                val = jnp.where(bkt == b, rb_ref[h, b], val)
            val = val - far
            if which == 0:
                val = jnp.where(dist >= 0, val, MASKED)
            bias_s[which] = val

    q = q_ref[...]
    lane = lax.broadcasted_iota(I32, q.shape, 1)
    qmaps = (jnp.where(lane < dk, q, jnp.zeros_like(q)), jnp.where(lane >= dk, q, jnp.zeros_like(q)))

    lam = _diff_lambda(lam_ref, lam_init)

    def attend(n):
        ks = [k_ref[j * blk:(j + 1) * blk, :] for j in range(n)]
        vts = [vt_ref[j] for j in range(n)]
        biases = [None] * (n - 2) + [bias_s[1], bias_s[0]][-min(n, 2):]
        outs = []
        for mp in range(2):
            scores = []
            for k, bias in zip(ks, biases):
                s = lax.dot_general(k, qmaps[mp], NT_DIMS, preferred_element_type=F32)
                scores.append(s if bias is None else s + bias)
            smax = scores[0]
            for s in scores[1:]:
                smax = jnp.maximum(smax, s)
            m = jnp.max(smax, axis=0, keepdims=True)
            probs = [jnp.exp(s - m) for s in scores]
            psum = probs[0]
            for p in probs[1:]:
                psum = psum + p
            pv = None
            for p, vt in zip(probs, vts):
                d = jnp.dot(vt, p.astype(BF16), preferred_element_type=F32)
                pv = d if pv is None else pv + d
            outs.append(pv / jnp.sum(psum, axis=0, keepdims=True))
        ot = outs[0] - lam * outs[1]
        o_ref[...] = (_rms(ot.T, gain_ref[...]) * (1.0 - lam_init)).astype(o_ref.dtype)

    for n in range(1, nq + 1):
        pl.when(qi == n - 1)(functools.partial(attend, n))


def _prompt_attention(q, k, vt, rel_bias_t, lam_vecs, gain, batch, seq, lam_init, blk):
    n_heads, n_buckets = rel_bias_t.shape
    dv = vt.shape[2]
    assert blk >= MAX_DISTANCE and seq % blk == 0
    nq = seq // blk
    return pl.pallas_call(
        functools.partial(_pattn_kernel, blk=blk, lam_init=lam_init, n_buckets=n_buckets, nq=nq),
        grid=(batch, n_heads, nq),
        in_specs=[pl.BlockSpec(memory_space=pltpu.SMEM),
                  pl.BlockSpec(lam_vecs.shape, lambda b, h, i: (0, 0)),
                  pl.BlockSpec((1, dv), lambda b, h, i: (0, 0)),
                  pl.BlockSpec((blk, dv), lambda b, h, i: (b * nq + i, h)),
                  pl.BlockSpec((seq, dv), lambda b, h, i: (b, h)),
                  pl.BlockSpec((None, nq, dv, blk), lambda b, h, i: (b * n_heads + h, 0, 0, 0))],
        out_specs=pl.BlockSpec((blk, dv), lambda b, h, i: (b * nq + i, h)),
        out_shape=jax.ShapeDtypeStruct(q.shape, BF16),
        scratch_shapes=[pltpu.VMEM((2, blk, blk), F32)],
        compiler_params=_cparams("arbitrary", "arbitrary", "arbitrary"),
        name="prompt_diff_attention",
    )(rel_bias_t, lam_vecs, gain.reshape(1, dv), q, k, vt)


def _sattn_kernel(pt_ref, rbt_ref, lam_ref, gain_ref, qb_ref, ks_ref, vs_ref, ck_hbm, cv_hbm, o_ref,
                  kbuf, vbuf, mb_s, mbs_s, m_s, l_s, acc_s, sem,
                  *, group, n_pages, n_batch, page_off, lam_init, n_buckets, n_heads, t_new):
    b = pl.program_id(0)
    n_groups = n_pages // group
    rows = qb_ref.shape[0]
    half = rows // 2
    cols = kbuf.shape[2]
    scols = ks_ref.shape[0]

    @pl.when(b == 0)
    def _():
        def lookup(dist, width):
            bkt = _t5_bucket(dist, n_buckets)
            val = jnp.zeros(dist.shape, F32)
            for i in range(n_buckets):
                val = jnp.where(bkt == i, rbt_ref[i:i + 1, 0:width], val)
            return val - rbt_ref[n_buckets - 1:n_buckets, 0:width]

        r = lax.broadcasted_iota(I32, (rows, cols), 0)
        c = lax.broadcasted_iota(I32, (rows, cols), 1)
        hrow = (r % half) // t_new
        trow = r % t_new
        valid = hrow == c % n_heads
        mb_s[0] = jnp.where(valid, 0.0, MASKED)
        mb_s[1] = jnp.where(valid, lookup(PAGE_SIZE + trow - c // n_heads, cols), MASKED)
        r2 = lax.broadcasted_iota(I32, (rows, scols), 0)
        c2 = lax.broadcasted_iota(I32, (rows, scols), 1)
        t2 = r2 % t_new
        key2 = c2 // n_heads
        valid2 = ((r2 % half) // t_new == c2 % n_heads) & (key2 <= t2)
        mbs_s[...] = jnp.where(valid2, lookup(t2 - key2, scols), MASKED)

    def copies(bb, g, slot):
        out = []
        for j in range(group):
            page = pt_ref[bb, g * group + j] + page_off
            out.append(pltpu.make_async_copy(ck_hbm.at[page], kbuf.at[slot, j], sem.at[slot, 0, j]))
            out.append(pltpu.make_async_copy(cv_hbm.at[page], vbuf.at[slot, j], sem.at[slot, 1, j]))
        return out

    @pl.when(b == 0)
    def _():
        for cp in copies(0, 0, 0):
            cp.start()

    m_s[...] = jnp.full(m_s.shape, -jnp.inf, F32)
    l_s[...] = jnp.zeros(l_s.shape, F32)
    acc_s[...] = jnp.zeros(acc_s.shape, F32)
    qb = qb_ref[...]

    def update(blocks):
        scores = [lax.dot_general(qb, kp, NT_DIMS, preferred_element_type=F32) + mb for kp, _, mb in blocks]
        smax = scores[0]
        for s in scores[1:]:
            smax = jnp.maximum(smax, s)
        m_old = m_s[...]
        m_new = jnp.maximum(m_old, jnp.max(smax, axis=-1, keepdims=True))
        corr = jnp.exp(m_old - m_new)
        probs = [jnp.exp(s - m_new) for s in scores]
        psum = probs[0]
        for p in probs[1:]:
            psum = psum + p
        pv = None
        for p, (_, vp, _) in zip(probs, blocks):
            d = jnp.dot(p.astype(BF16), vp, preferred_element_type=F32)
            pv = d if pv is None else pv + d
        l_s[...] = l_s[...] * corr + jnp.sum(psum, axis=-1, keepdims=True)
        acc_s[...] = acc_s[...] * corr + pv
        m_s[...] = m_new

    @pl.loop(0, n_groups)
    def _(g):
        slot = g % 2
        nxt = g + 1

        @pl.when(nxt < n_groups)
        def _():
            for cp in copies(b, nxt, 1 - slot):
                cp.start()

        @pl.when((nxt == n_groups) & (b + 1 < n_batch))
        def _():
            for cp in copies(b + 1, 0, 1 - slot):
                cp.start()

        for cp in copies(b, g, slot):
            cp.wait()
        last = mb_s[jnp.where(g == n_groups - 1, 1, 0)]
        update([(kbuf[slot, j].astype(BF16), vbuf[slot, j].astype(BF16), last if j == group - 1 else mb_s[0])
                for j in range(group)])

    update([(ks_ref[...], vs_ref[...], mbs_s[...])])

    lam = _diff_lambda(lam_ref, lam_init)
    o = acc_s[...] / l_s[...]
    od = o[0:half] - lam * o[half:rows]
    o_ref[...] = (_rms(od, gain_ref[...]) * (1.0 - lam_init)).astype(o_ref.dtype)


def _sample_attention(qb, ks, vs, cache_k2, cache_v2, page_table, page_off, rbt, lam_vecs, gain, lam_init,
                      n_heads, t_new, group=8):
    n_batch, rows, dv = qb.shape
    n_pages = page_table.shape[1]
    cols = cache_k2.shape[1]
    n_buckets = rbt.shape[0]
    assert n_pages % group == 0 and (n_pages // group) % 2 == 0
    kern = functools.partial(_sattn_kernel, group=group, n_pages=n_pages, n_batch=n_batch, page_off=page_off,
                             lam_init=lam_init, n_buckets=n_buckets, n_heads=n_heads, t_new=t_new)
    grid_spec = pltpu.PrefetchScalarGridSpec(
        num_scalar_prefetch=1,
        grid=(n_batch,),
        in_specs=[pl.BlockSpec(rbt.shape, lambda b, pt: (0, 0)),
                  pl.BlockSpec(lam_vecs.shape, lambda b, pt: (0, 0)),
                  pl.BlockSpec((1, dv), lambda b, pt: (0, 0)),
                  pl.BlockSpec((None, rows, dv), lambda b, pt: (b, 0, 0)),
                  pl.BlockSpec((None,) + ks.shape[1:], lambda b, pt: (b, 0, 0)),
                  pl.BlockSpec((None,) + vs.shape[1:], lambda b, pt: (b, 0, 0)),
                  pl.BlockSpec(memory_space=pl.ANY),
                  pl.BlockSpec(memory_space=pl.ANY)],
        out_specs=pl.BlockSpec((None, rows // 2, dv), lambda b, pt: (b, 0, 0)),
        scratch_shapes=[pltpu.VMEM((2, group, cols, dv), F32), pltpu.VMEM((2, group, cols, dv), F32),
                        pltpu.VMEM((2, rows, cols), F32), pltpu.VMEM((rows, ks.shape[1]), F32),
                        pltpu.VMEM((rows, 1), F32), pltpu.VMEM((rows, 1), F32), pltpu.VMEM((rows, dv), F32),
                        pltpu.SemaphoreType.DMA((2, 2, group))])
    return pl.pallas_call(
        kern, grid_spec=grid_spec,
        out_shape=jax.ShapeDtypeStruct((n_batch, rows // 2, dv), BF16),
        compiler_params=_cparams("arbitrary"),
        name="sample_diff_attention",
    )(page_table, rbt, lam_vecs, gain.reshape(1, dv), qb, ks, vs, cache_k2, cache_v2)


def _rg_gates(xc, wrg_ref, brg_ref, wig_ref, big_ref, lam_ref):
    n_blocks, blk = wrg_ref.shape[0], wrg_ref.shape[1]
    rs, gs = [], []
    for n in range(n_blocks):
        xb = xc[:, n * blk:(n + 1) * blk].astype(BF16)
        rs.append(jnp.dot(xb, wrg_ref[n].astype(BF16), preferred_element_type=F32))
        gs.append(jnp.dot(xb, wig_ref[n].astype(BF16), preferred_element_type=F32))
    r = jax.nn.sigmoid(jnp.concatenate(rs, axis=-1) + brg_ref[...])
    i = jax.nn.sigmoid(jnp.concatenate(gs, axis=-1) + big_ref[...])
    log_a = (-RG_C * _softplus(-lam_ref[...])) * r
    a = jnp.exp(log_a)
    th = jnp.tanh(log_a)
    u = xc * i * jnp.sqrt(-2.0 * th / (1.0 - th))
    return a, u


def _rglru_prompt_kernel(x_ref, g_ref, wc_ref, bc_ref, wrg_ref, brg_ref, wig_ref, big_ref, lam_ref,
                         hr_ref, hl_ref, xbuf, a_s, u_s, h_s, *, conv_w):
    i = pl.program_id(1)
    tc = x_ref.shape[0]
    d = x_ref.shape[1]

    @pl.when(i == 0)
    def _():
        xbuf[0:SUBLANES] = jnp.zeros((SUBLANES, d), F32)
        h_s[...] = jnp.zeros(h_s.shape, F32)

    xbuf[SUBLANES:SUBLANES + tc] = x_ref[...]
    xc = bc_ref[...]
    for j in range(conv_w):
        lo = SUBLANES - (conv_w - 1) + j
        xc = xc + wc_ref[j:j + 1] * xbuf[lo:lo + tc]
    xbuf[0:SUBLANES] = xbuf[tc:tc + SUBLANES]

    a, u = _rg_gates(xc, wrg_ref, brg_ref, wig_ref, big_ref, lam_ref)
    a_s[...] = a
    u_s[...] = u
    row = lax.broadcasted_iota(I32, (SUBLANES, d), 0)

    @pl.loop(0, tc // SUBLANES)
    def _(g):
        st = pl.multiple_of(g * SUBLANES, SUBLANES)
        a8 = a_s[pl.ds(st, SUBLANES), :]
        u8 = u_s[pl.ds(st, SUBLANES), :]
        for s in (1, 2, 4):
            a_prev = jnp.where(row >= s, pltpu.roll(a8, s, 0), 1.0)
            u_prev = jnp.where(row >= s, pltpu.roll(u8, s, 0), 0.0)
            u8 = a8 * u_prev + u8
            a8 = a8 * a_prev
        h8 = a8 * h_s[...] + u8
        h_s[...] = jnp.broadcast_to(h8[SUBLANES - 1:SUBLANES], (SUBLANES, d))
        u_s[pl.ds(st, SUBLANES), :] = h8

    hr_ref[...] = (u_s[...] * _gelu_tanh(g_ref[...])).astype(hr_ref.dtype)

    @pl.when(i == pl.num_programs(1) - 1)
    def _():
        hl_ref[...] = h_s[0:1]


def _rglru_prompt(rg, w_conv, b_conv, w_rg, b_rg, w_ig, b_ig, lam_rg, batch, seq, tc=256):
    d = rg.shape[1] // 2
    nt = seq // tc
    conv_w = w_conv.shape[0]
    assert conv_w - 1 <= SUBLANES and seq % tc == 0
    vec = pl.BlockSpec((1, d), lambda b, i: (0, 0))
    wblk = pl.BlockSpec(w_rg.shape, lambda b, i: (0, 0, 0))
    return pl.pallas_call(
        functools.partial(_rglru_prompt_kernel, conv_w=conv_w),
        grid=(batch, nt),
        in_specs=[pl.BlockSpec((tc, d), lambda b, i: (b * nt + i, 0)),
                  pl.BlockSpec((tc, d), lambda b, i: (b * nt + i, 1)),
                  pl.BlockSpec((conv_w, d), lambda b, i: (0, 0)), vec, wblk, vec, wblk, vec, vec],
        out_specs=[pl.BlockSpec((tc, d), lambda b, i: (b * nt + i, 0)),
                   pl.BlockSpec((None, 1, d), lambda b, i: (b, 0, 0))],
        out_shape=[jax.ShapeDtypeStruct((batch * seq, d), BF16), jax.ShapeDtypeStruct((batch, 1, d), F32)],
        scratch_shapes=[pltpu.VMEM((tc + SUBLANES, d), F32), pltpu.VMEM((tc, d), F32),
                        pltpu.VMEM((tc, d), F32), pltpu.VMEM((SUBLANES, d), F32)],
        compiler_params=_cparams("arbitrary", "arbitrary"),
        name="rglru_prompt",
    )(rg, rg, w_conv, b_conv.reshape(1, d), w_rg, b_rg.reshape(1, d), w_ig, b_ig.reshape(1, d),
      lam_rg.reshape(1, d))


def _rglru_sample_kernel(x_ref, g_ref, cs_ref, h0_ref, wc_ref, bc_ref, wrg_ref, brg_ref, wig_ref, big_ref,
                         lam_ref, hr_ref, hl_ref, *, conv_w):
    t_new = x_ref.shape[0]
    buf = [cs_ref[j] for j in range(conv_w - 1)] + [x_ref[t] for t in range(t_new)]
    h = h0_ref[...]
    for t in range(t_new):
        xc = bc_ref[...]
        for j in range(conv_w):
            xc = xc + wc_ref[j:j + 1] * buf[t + j]
        a, u = _rg_gates(xc, wrg_ref, brg_ref, wig_ref, big_ref, lam_ref)
        h = a * h + u
        hr_ref[t] = (h * _gelu_tanh(g_ref[t])).astype(hr_ref.dtype)
    hl_ref[...] = h


def _rglru_sample(x_t, g_t, conv_t, h0, w_conv, b_conv, w_rg, b_rg, w_ig, b_ig, lam_rg):
    t_new, batch, d = x_t.shape
    return pl.pallas_call(
        functools.partial(_rglru_sample_kernel, conv_w=w_conv.shape[0]),
        out_shape=[jax.ShapeDtypeStruct((t_new, batch, d), BF16), jax.ShapeDtypeStruct((batch, d), F32)],
        compiler_params=pltpu.CompilerParams(vmem_limit_bytes=V7X_VMEM_LIMIT_BYTES),
        name="rglru_sample",
    )(x_t, g_t, conv_t, h0, w_conv, b_conv.reshape(1, d), w_rg, b_rg.reshape(1, d), w_ig, b_ig.reshape(1, d),
      lam_rg.reshape(1, d))


def _router_kernel(xp_ref, xs_ref, gain_ref, sc_ref, sh_ref, wr_ref, br_ref, hn_ref, idx_ref, gate_ref, rank_ref,
                   cnt_ref, base_s, *, n_prompt_tiles):
    i = pl.program_id(0)
    tm = xp_ref.shape[0]
    n_exp = wr_ref.shape[1]

    @pl.when(i == 0)
    def _():
        base_s[...] = jnp.zeros(base_s.shape, F32)

    x = jnp.where(i < n_prompt_tiles, xp_ref[...], xs_ref[...])
    hn = _rms(x, gain_ref[...]) * (1.0 + sc_ref[...]) + sh_ref[...]
    hb = hn.astype(BF16)
    bits = lax.bitcast_convert_type(hb.astype(F32), U32)
    half = bits.shape[1] // 2
    hn_ref[...] = (bits[:, half:] & HIGH_HALF) | (bits[:, :half] >> 16)
    logits = jnp.dot(hb, wr_ref[...].astype(BF16), preferred_element_type=F32) + br_ref[...]

    lane = lax.broadcasted_iota(I32, (tm, n_exp), 1)
    work = logits
    sels, vals = [], []
    for _ in range(TOP_K):
        mx = jnp.max(work, axis=-1, keepdims=True)
        first = jnp.min(jnp.where(work == mx, lane, n_exp), axis=-1, keepdims=True)
        sel = lane == first
        sels.append(sel)
        vals.append(mx)
        idx_ref[:, len(sels) - 1:len(sels)] = first
        work = jnp.where(sel, -jnp.inf, work)

    es = [jnp.exp(v - vals[0]) for v in vals]
    tot = es[0]
    for e in es[1:]:
        tot = tot + e
    for k in range(TOP_K):
        gate_ref[:, k:k + 1] = es[k] / tot

    chosen = sels[0]
    for sel in sels[1:]:
        chosen = chosen | sel
    onehot = jnp.where(chosen, 1.0, 0.0)
    r = lax.broadcasted_iota(I32, (tm, tm), 0)
    c = lax.broadcasted_iota(I32, (tm, tm), 1)
    tri = jnp.where(r > c, 1.0, 0.0).astype(BF16)
    pos = jnp.dot(tri, onehot.astype(BF16), preferred_element_type=F32) + base_s[...]
    for k in range(TOP_K):
        rank_ref[:, k:k + 1] = jnp.sum(jnp.where(sels[k], pos, 0.0), axis=-1, keepdims=True).astype(I32)
    base_s[...] = base_s[...] + jnp.sum(onehot, axis=0, keepdims=True)
    cnt_ref[...] = base_s[...].astype(I32)


def _router(x1p, x1s, gain, sc, sh, w_router, b_router, tiles_per_group):
    tm, d = x1s.shape
    npt = x1p.shape[0] // tm
    r = x1p.shape[0] + tm
    n_exp = w_router.shape[1]
    n_groups = sc.shape[0]
    mod_spec = pl.BlockSpec((None, tm, d), lambda i: (jnp.minimum(i // tiles_per_group, n_groups - 1), 0, 0))
    small = pl.BlockSpec((tm, TOP_K), lambda i: (i, 0))
    return pl.pallas_call(
        functools.partial(_router_kernel, n_prompt_tiles=npt),
        grid=(r // tm,),
        in_specs=[pl.BlockSpec((tm, d), lambda i: (jnp.minimum(i, npt - 1), 0)),
                  pl.BlockSpec((tm, d), lambda i: (0, 0)), pl.BlockSpec((1, d), lambda i: (0, 0)),
                  mod_spec, mod_spec,
                  pl.BlockSpec((d, n_exp), lambda i: (0, 0)), pl.BlockSpec((1, n_exp), lambda i: (0, 0))],
        out_specs=[pl.BlockSpec((tm, d // 2), lambda i: (i, 0)), small, small, small,
                   pl.BlockSpec((1, n_exp), lambda i: (0, 0))],
        out_shape=[jax.ShapeDtypeStruct((r, d // 2), U32), jax.ShapeDtypeStruct((r, TOP_K), I32),
                   jax.ShapeDtypeStruct((r, TOP_K), F32), jax.ShapeDtypeStruct((r, TOP_K), I32),
                   jax.ShapeDtypeStruct((1, n_exp), I32)],
        scratch_shapes=[pltpu.VMEM((1, n_exp), F32)],
        compiler_params=_cparams("arbitrary"),
        name="router_topk",
    )(x1p, x1s, gain.reshape(1, d), sc, sh, w_router, b_router.reshape(1, n_exp))


def _expert_kernel(te_ref, rows_ref, nu_ref, src_ref, hn_hbm, wg_ref, wu_ref, bg_ref, bu_ref, wd_ref, bd_ref, o_ref,
                   xg, xb, wgb, wub, wdb, sem, *, big, chunk):
    t = pl.program_id(0)
    f = pl.program_id(1)
    tb = xb.shape[0]
    n_used = nu_ref[0]

    def n_chunks(tile):
        return (rows_ref[tile] + chunk - 1) // chunk

    def gather(tile):
        base = tile * tb

        @pl.loop(0, n_chunks(tile) * (chunk // GATHER_UNROLL))
        def _(i):
            r0 = i * GATHER_UNROLL
            for j in range(GATHER_UNROLL):
                tok = src_ref[base + r0 + j]
                pltpu.make_async_copy(hn_hbm.at[pl.ds(tok, 1)], xg.at[pl.ds(r0 + j, 1)], sem.at[0]).start()

    @pl.when((t == 0) & (f == 0))
    def _():
        gather(0)

    @pl.when((t < n_used) & (f == 0))
    def _():
        @pl.loop(0, n_chunks(t))
        def _(c):
            pltpu.make_async_copy(hn_hbm.at[pl.ds(0, chunk)], xg.at[pl.ds(0, chunk)], sem.at[0]).wait()

        half = xg.shape[1]

        @pl.loop(0, n_chunks(t))
        def _(c):
            rs = pl.ds(pl.multiple_of(c * chunk, chunk), chunk)
            w = xg[rs, :]
            xb[rs, 0:half] = lax.bitcast_convert_type(w << 16, F32).astype(BF16)
            xb[rs, half:2 * half] = lax.bitcast_convert_type(w & HIGH_HALF, F32).astype(BF16)

        o_ref[...] = jnp.broadcast_to(bd_ref[...], o_ref.shape)

        @pl.when(t + 1 < n_used)
        def _():
            gather(t + 1)

    @pl.when(t < n_used)
    def _():
        wgb[...] = wg_ref[...].astype(BF16)
        wub[...] = wu_ref[...].astype(BF16)
        wdb[...] = wd_ref[...].astype(BF16)

        def ffn(start, size):
            rs = pl.ds(pl.multiple_of(start, chunk), size)
            x = xb[rs, :]
            g = jnp.dot(x, wgb[...], preferred_element_type=F32) + bg_ref[...]
            u = jnp.dot(x, wub[...], preferred_element_type=F32) + bu_ref[...]
            g = jnp.minimum(g, SWIGLU_LIMIT)
            u = jnp.clip(u, -SWIGLU_LIMIT, SWIGLU_LIMIT)
            act = g * jax.nn.sigmoid(SWIGLU_ALPHA * g) * (u + 1.0)
            o_ref[rs, :] += jnp.dot(act.astype(BF16), wdb[...], preferred_element_type=F32)

        done = 0
        left = rows_ref[t]
        size = big
        while size > chunk:
            n = left // size if size == big else jnp.minimum(left // size, 1)

            @pl.loop(0, n)
            def _(i, done=done, size=size):
                ffn(done + i * size, size)

            done = done + n * size
            left = left - n * size
            size //= 2

        @pl.loop(0, (left + chunk - 1) // chunk)
        def _(i, done=done):
            ffn(done + i * chunk, chunk)

    @pl.when((t >= n_used) & (f == 0))
    def _():
        o_ref[...] = jnp.zeros(o_ref.shape, F32)


def _experts(hn, tile_expert, tile_rows, n_used, src_token, w_up, b_up, w_down, b_down, tb, tf, big, chunk):
    n_exp, d, ff2 = w_up.shape
    ff = ff2 // 2
    nf = ff // tf
    n_tiles = tile_expert.shape[0]
    assert tb % chunk == 0 and big % chunk == 0 and chunk % GATHER_UNROLL == 0

    def eff(t, f, nu):
        return jnp.minimum(t, nu[0] - 1), jnp.where(t < nu[0], f, nf - 1)

    def wg_map(t, f, te, rows, nu, src):
        tt, fe = eff(t, f, nu)
        return te[tt], 0, fe

    def wu_map(t, f, te, rows, nu, src):
        tt, fe = eff(t, f, nu)
        return te[tt], 0, fe + nf

    def wd_map(t, f, te, rows, nu, src):
        tt, fe = eff(t, f, nu)
        return te[tt], fe, 0

    def bd_map(t, f, te, rows, nu, src):
        tt, _ = eff(t, f, nu)
        return te[tt], 0, 0

    def out_map(t, f, te, rows, nu, src):
        return t, 0

    grid_spec = pltpu.PrefetchScalarGridSpec(
        num_scalar_prefetch=4,
        grid=(n_tiles, nf),
        in_specs=[pl.BlockSpec(memory_space=pl.ANY),
                  pl.BlockSpec((None, d, tf), wg_map), pl.BlockSpec((None, d, tf), wu_map),
                  pl.BlockSpec((None, 1, tf), wg_map), pl.BlockSpec((None, 1, tf), wu_map),
                  pl.BlockSpec((None, tf, d), wd_map), pl.BlockSpec((None, 1, d), bd_map)],
        out_specs=pl.BlockSpec((tb, d), out_map),
        scratch_shapes=[pltpu.VMEM((tb, d // 2), U32), pltpu.VMEM((tb, d), BF16),
                        pltpu.VMEM((d, tf), BF16), pltpu.VMEM((d, tf), BF16), pltpu.VMEM((tf, d), BF16),
                        pltpu.SemaphoreType.DMA((1,))])
    return pl.pallas_call(
        functools.partial(_expert_kernel, big=big, chunk=chunk), grid_spec=grid_spec,
        out_shape=jax.ShapeDtypeStruct((n_tiles * tb, d), F32),
        compiler_params=_cparams("arbitrary", "arbitrary"),
        name="routed_experts",
    )(tile_expert, tile_rows, n_used, src_token, hn, w_up, w_up, b_up.reshape(n_exp, 1, ff2),
      b_up.reshape(n_exp, 1, ff2), w_down, b_down.reshape(n_exp, 1, d))


def _combine_kernel(dest_ref, ys_hbm, xp_ref, xs_ref, gate_ref, g2_ref, fn_ref, yp_ref, ysm_ref, buf, sem,
                    *, n_prompt_tiles):
    i = pl.program_id(0)
    n = pl.num_programs(0)
    tm = xp_ref.shape[0]
    slot = i % 2

    def gather(tile, into):
        base = tile * tm * TOP_K
        per_iter = GATHER_UNROLL // TOP_K

        @pl.loop(0, tm // per_iter)
        def _(it):
            for j in range(per_iter):
                r = it * per_iter + j
                for k in range(TOP_K):
                    row = dest_ref[base + r * TOP_K + k]
                    pltpu.make_async_copy(ys_hbm.at[pl.ds(row, 1)], buf.at[into, k, pl.ds(r, 1)],
                                          sem.at[into]).start()

    @pl.when(i == 0)
    def _():
        gather(0, 0)

    @pl.when(i + 1 < n)
    def _():
        gather(i + 1, 1 - slot)

    for k in range(TOP_K):
        pltpu.make_async_copy(ys_hbm.at[pl.ds(0, tm)], buf.at[slot, k], sem.at[slot]).wait()

    moe = gate_ref[:, 0:1] * buf[slot, 0]
    for k in range(1, TOP_K):
        moe = moe + gate_ref[:, k:k + 1] * buf[slot, k]
    x = jnp.where(i < n_prompt_tiles, xp_ref[...], xs_ref[...])
    y = _rms(x + g2_ref[...] * moe, fn_ref[...])

    @pl.when(i < n_prompt_tiles)
    def _():
        yp_ref[...] = y

    @pl.when(i >= n_prompt_tiles)
    def _():
        ysm_ref[...] = y


def _combine(dest_flat, ys, x1p, x1s, gate, g2, final_norm, tiles_per_group):
    tm, d = x1s.shape
    n_prompt_rows = x1p.shape[0]
    r = n_prompt_rows + tm
    n_groups = g2.shape[0]
    npt = n_prompt_rows // tm
    grid_spec = pltpu.PrefetchScalarGridSpec(
        num_scalar_prefetch=1,
        grid=(r // tm,),
        in_specs=[pl.BlockSpec(memory_space=pl.ANY),
                  pl.BlockSpec((tm, d), lambda i, dst: (jnp.minimum(i, npt - 1), 0)),
                  pl.BlockSpec((tm, d), lambda i, dst: (0, 0)),
                  pl.BlockSpec((tm, TOP_K), lambda i, dst: (i, 0)),
                  pl.BlockSpec((None, tm, d), lambda i, dst: (jnp.minimum(i // tiles_per_group, n_groups - 1), 0, 0)),
                  pl.BlockSpec((1, d), lambda i, dst: (0, 0))],
        out_specs=[pl.BlockSpec((tm, d), lambda i, dst: (jnp.minimum(i, npt - 1), 0)),
                   pl.BlockSpec((tm, d), lambda i, dst: (0, 0))],
        scratch_shapes=[pltpu.VMEM((2, TOP_K, tm, d), F32), pltpu.SemaphoreType.DMA((2,))])
    return pl.pallas_call(
        functools.partial(_combine_kernel, n_prompt_tiles=npt), grid_spec=grid_spec,
        out_shape=[jax.ShapeDtypeStruct((n_prompt_rows, d), F32), jax.ShapeDtypeStruct((tm, d), F32)],
        compiler_params=_cparams("arbitrary"),
        name="expert_combine_final_norm",
    )(dest_flat, ys, x1p, x1s, gate, g2, final_norm.reshape(1, d))


def kernel(x_prompt, x_sample, c_prompt, c_sample, cache_k, cache_v, state_conv, state_h, page_table, rel_bias, w_ada, b_ada, norm_mix, norm_ffn, w_in, lam_q1, lam_k1, lam_q2, lam_k2, subln_gain, w_conv, b_conv, w_rg, b_rg, w_ig, b_ig, lam_rg, w_att_out, w_rnn_out, w_o, w_router, b_router, w_up, b_up, w_down, b_down, final_norm):
    bp, seq, d = x_prompt.shape
    bs, t_new, _ = x_sample.shape
    depth, n_pool, page, n_heads, dv = cache_v.shape
    assert page == PAGE_SIZE and cache_k.shape[-1] == dv
    d_rnn = w_conv.shape[-1]
    qk_w = n_heads * dv
    n_exp = w_router.shape[-1]
    rp, rs = bp * seq, bs * t_new
    scale = (dv // 2) ** -0.5
    assert depth == 1
    tm_p, tm_s = PROMPT_ROW_TILE, rs
    tok_tile = rs
    assert seq % tm_p == 0 and rp % tok_tile == 0 and rs % SUBLANES == 0
    expert_rows, ff_tile, tn = EXPERT_ROW_TILE, FF_TILE, WEIGHT_COL_TILE

    xp = x_prompt.reshape(rp, d)
    xs = x_sample.reshape(rs, d)
    c_all = jnp.concatenate([c_prompt, c_sample, jnp.zeros((-(bp + bs) % SUBLANES, d), F32)], axis=0)
    cache_k2 = cache_k.reshape(depth * n_pool, page * n_heads, dv)
    cache_v2 = cache_v.reshape(depth * n_pool, page * n_heads, dv)
    rel_bias_t = rel_bias.T
    rbt = jnp.tile(rel_bias, (1, page))

    outs = {k: [] for k in ("kp", "vp", "cp", "hp", "ks", "vs", "cs", "hs")}
    for l in range(depth):
        lam_init = 0.8 - 0.6 * math.exp(-0.3 * l)
        lam_vecs = jnp.stack([lam_q1[l], lam_k1[l], lam_q2[l], lam_k2[l]])
        mod = _ada(c_all, w_ada[l], b_ada[l])
        mods_p = [m.reshape(bp, 1, d) for m in jnp.split(mod[:bp], 6, axis=-1)]
        mods_s = [jnp.repeat(m, t_new, axis=0).reshape(1, rs, d) for m in jnp.split(mod[bp:bp + bs], 6, axis=-1)]

        x1 = []
        for grp, (x, mods, tm, tpg) in enumerate(((xp, mods_p, tm_p, seq // tm_p), (xs, mods_s, tm_s, 1))):
            sh1, sc1, g1 = mods[0], mods[1], mods[2]
            hn = _prenorm(x, norm_mix[l], sc1, sh1, tm, tpg)
            (q,) = _proj(hn, w_in[l], 0, qk_w, [BF16], tm, tn, scale=scale, name="proj_q")
            k32, kbf = _proj(hn, w_in[l], qk_w, qk_w, [F32, BF16], tm, tn, name="proj_k")
            v32, vbf = _proj(hn, w_in[l], 2 * qk_w, qk_w, [F32, BF16], tm, tn, name="proj_v")
            (rg,) = _proj(hn, w_in[l], 3 * qk_w, 2 * d_rnn, [F32], tm, tn, name="proj_rnn")
            (gates,) = _proj(hn, w_in[l], 3 * qk_w + 2 * d_rnn, 2 * d, [F32], tm, tn, name="proj_gates")

            if grp == 0:
                nkb = seq // ATTN_BLOCK
                vt = vbf.reshape(bp, nkb, ATTN_BLOCK, n_heads, dv).transpose(0, 3, 1, 4, 2)
                vt = vt.reshape(bp * n_heads, nkb, dv, ATTN_BLOCK)
                o_att = _prompt_attention(q, kbf, vt, rel_bias_t, lam_vecs, subln_gain[l], bp, seq, lam_init,
                                          ATTN_BLOCK)
                hr, h_last = _rglru_prompt(rg, w_conv[l], b_conv[l], w_rg[l], b_rg[l], w_ig[l], b_ig[l], lam_rg[l],
                                           bp, seq)
                xr3 = rg[:, :d_rnn].reshape(bp, seq, d_rnn)
                outs["kp"].append(k32.reshape(bp, seq, n_heads, dv))
                outs["vp"].append(v32.reshape(bp, seq, n_heads, dv))
                outs["cp"].append(xr3[:, seq - (w_conv.shape[1] - 1):])
                outs["hp"].append(h_last.reshape(bp, d_rnn))
            else:
                dk = dv // 2
                q5 = q.reshape(bs, t_new, n_heads, 2, dk).transpose(0, 3, 2, 1, 4)
                zeros = jnp.zeros_like(q5[:, 0])
                qb = jnp.stack([jnp.concatenate([q5[:, 0], zeros], axis=-1),
                                jnp.concatenate([zeros, q5[:, 1]], axis=-1)], axis=1)
                qb = qb.reshape(bs, 2 * n_heads * t_new, dv)
                pad = ((0, 0), (0, PAGE_SIZE - t_new * n_heads), (0, 0))
                k_self = jnp.pad(kbf.reshape(bs, t_new * n_heads, dv), pad)
                v_self = jnp.pad(vbf.reshape(bs, t_new * n_heads, dv), pad)
                o_s = _sample_attention(qb, k_self, v_self, cache_k2, cache_v2, page_table, l * n_pool, rbt,
                                        lam_vecs, subln_gain[l], lam_init, n_heads, t_new)
                o_att = o_s.reshape(bs, n_heads, t_new, dv).transpose(0, 2, 1, 3).reshape(rs, n_heads * dv)
                x_t = rg[:, :d_rnn].reshape(bs, t_new, d_rnn).transpose(1, 0, 2)
                g_t = rg[:, d_rnn:].reshape(bs, t_new, d_rnn).transpose(1, 0, 2)
                conv_t = state_conv[l].transpose(1, 0, 2)
                hr_t, h_last = _rglru_sample(x_t, g_t, conv_t, state_h[l], w_conv[l], b_conv[l], w_rg[l], b_rg[l],
                                             w_ig[l], b_ig[l], lam_rg[l])
                hr = hr_t.transpose(1, 0, 2).reshape(rs, d_rnn)
                buf = jnp.concatenate([state_conv[l], rg[:, :d_rnn].reshape(bs, t_new, d_rnn)], axis=1)
                outs["ks"].append(k32.reshape(bs, t_new, n_heads, dv))
                outs["vs"].append(v32.reshape(bs, t_new, n_heads, dv))
                outs["cs"].append(buf[:, t_new:])
                outs["hs"].append(h_last)

            mixed = _mix(o_att, hr, w_att_out[l], w_rnn_out[l], gates, tm, tn)
            x1.append(_resid(mixed, w_o[l], x, g1, tm, tn, tpg))

        n_tok = rp + rs
        tpg2 = seq // tok_tile

        def per_tile(mp, ms):
            return jnp.concatenate([jnp.broadcast_to(mp, (bp, tok_tile, d)), ms], axis=0)

        sh2, sc2, g2 = (per_tile(mods_p[j], mods_s[j]) for j in (3, 4, 5))
        hn2, top_idx, gate, rank, counts = _router(x1[0], x1[1], norm_ffn[l], sc2, sh2, w_router[l], b_router[l],
                                                   tpg2)
        counts = counts[0]
        padded = (counts + expert_rows - 1) // expert_rows * expert_rows
        pad_end = jnp.cumsum(padded)
        pad_start = pad_end - padded
        dest = (pad_start[top_idx] + rank).reshape(-1)
        n_tiles = (n_tok * TOP_K) // expert_rows + n_exp
        n_used = (pad_end[-1] // expert_rows).astype(I32).reshape(1)
        tile_first = jnp.arange(n_tiles, dtype=I32) * expert_rows
        tile_expert = jnp.minimum(jnp.searchsorted(pad_end, tile_first, side="right"), n_exp - 1).astype(I32)
        tile_rows = jnp.clip(counts[tile_expert] - (tile_first - pad_start[tile_expert]), 0, expert_rows).astype(I32)
        src_token = jnp.zeros((n_tiles * expert_rows,), I32).at[dest].set(
            jnp.arange(n_tok * TOP_K, dtype=I32) // TOP_K)
        ys = _experts(hn2, tile_expert, tile_rows, n_used, src_token, w_up[l], b_up[l], w_down[l], b_down[l],
                      expert_rows, ff_tile, EXPERT_BIG_ROWS, GATHER_CHUNK)
        yp, ysm = _combine(dest.astype(I32), ys, x1[0], x1[1], gate, g2, final_norm, tpg2)
        xp, xs = yp, ysm

    return (xp.reshape(bp, seq, d), xs.reshape(bs, t_new, d),
            jnp.stack(outs["kp"]), jnp.stack(outs["vp"]), jnp.stack(outs["cp"]), jnp.stack(outs["hp"]),
            jnp.stack(outs["ks"]), jnp.stack(outs["vs"]), jnp.stack(outs["cs"]), jnp.stack(outs["hs"]))
```

```python
import functools
import math

import jax
import jax.numpy as jnp
import numpy as np
from jax import lax
from jax.experimental import pallas as pl
from jax.experimental.pallas import tpu as pltpu

F32, BF16, I32, U32 = jnp.float32, jnp.bfloat16, jnp.int32, jnp.uint32
HIGH_HALF = np.uint32(0xFFFF0000)

TOP_K = 4
RG_C = 8.0
SWIGLU_LIMIT = 7.0
SWIGLU_ALPHA = 1.702
NORM_EPS = 1e-6
MAX_DISTANCE = 128
PAGE_SIZE = 128
MASKED = -1e30

V7X_VMEM_LIMIT_BYTES = 56 * 1024 * 1024
SUBLANES = 8

PROMPT_ROW_TILE = 1024
WEIGHT_COL_TILE = 1024
EXPERT_ROW_TILE = 1280
EXPERT_BIG_ROWS = 1024
GATHER_CHUNK = 64
GATHER_UNROLL = 8
EXPERT_BLOCK_MIN_COST = 1200.0
EXPERT_BLOCK_ROW_COST = 6.6
FF_TILE = 256
ATTN_BLOCK = 256

NT_DIMS = (((1,), (1,)), ((), ()))


def _cparams(*sem):
    return pltpu.CompilerParams(dimension_semantics=sem, vmem_limit_bytes=V7X_VMEM_LIMIT_BYTES)


def _rms(x, gain):
    return x * lax.rsqrt(jnp.mean(x * x, axis=-1, keepdims=True) + NORM_EPS) * gain


def _gelu_tanh(x):
    return 0.5 * x * (1.0 + jnp.tanh(math.sqrt(2.0 / math.pi) * (x + 0.044715 * (x * x * x))))


def _softplus(x):
    return jnp.maximum(x, 0.0) + jnp.log1p(jnp.exp(-jnp.abs(x)))


def _t5_bucket(dist, n_buckets):
    n = jnp.maximum(dist, 0)
    max_exact = n_buckets // 2
    nf = jnp.maximum(n, 1).astype(F32)
    large = max_exact + (jnp.log(nf / max_exact) / math.log(MAX_DISTANCE / max_exact)
                         * (n_buckets - max_exact)).astype(I32)
    large = jnp.minimum(large, n_buckets - 1)
    return jnp.where(n < max_exact, n, large)


def _diff_lambda(lam_ref, lam_init):
    lq = lam_ref[...]
    s1 = jnp.sum(lq[0:1] * lq[1:2], axis=-1, keepdims=True)
    s2 = jnp.sum(lq[2:3] * lq[3:4], axis=-1, keepdims=True)
    return jnp.exp(s1) - jnp.exp(s2) + lam_init


def _ada_kernel(c_ref, w_ref, b_ref, o_ref):
    c = c_ref[...]
    s = (c * jax.nn.sigmoid(c)).astype(BF16)
    o_ref[...] = jnp.dot(s, w_ref[...].astype(BF16), preferred_element_type=F32) + b_ref[...]


def _ada(c, w, b, tn=WEIGHT_COL_TILE):
    m, d = c.shape
    n = w.shape[1]
    return pl.pallas_call(
        _ada_kernel,
        grid=(n // tn,),
        in_specs=[pl.BlockSpec((m, d), lambda j: (0, 0)),
                  pl.BlockSpec((d, tn), lambda j: (0, j)),
                  pl.BlockSpec((1, tn), lambda j: (0, j))],
        out_specs=pl.BlockSpec((m, tn), lambda j: (0, j)),
        out_shape=jax.ShapeDtypeStruct((m, n), F32),
        compiler_params=_cparams("arbitrary"),
        name="adaln",
    )(c, w, b.reshape(1, n))


def _prenorm_kernel(x_ref, g_ref, sc_ref, sh_ref, o_ref):
    y = _rms(x_ref[...], g_ref[...])
    o_ref[...] = (y * (1.0 + sc_ref[...]) + sh_ref[...]).astype(o_ref.dtype)


def _prenorm(x, gain, sc, sh, tm, tiles_per_group):
    r, d = x.shape
    rm = sc.shape[1]
    mod_spec = pl.BlockSpec((None, rm, d), lambda i: (i // tiles_per_group, 0, 0))
    return pl.pallas_call(
        _prenorm_kernel,
        grid=(r // tm,),
        in_specs=[pl.BlockSpec((tm, d), lambda i: (i, 0)),
                  pl.BlockSpec((1, d), lambda i: (0, 0)), mod_spec, mod_spec],
        out_specs=pl.BlockSpec((tm, d), lambda i: (i, 0)),
        out_shape=jax.ShapeDtypeStruct((r, d), BF16),
        compiler_params=_cparams("arbitrary"),
        name="prenorm_mix",
    )(x, gain.reshape(1, d), sc, sh)


def _proj_kernel(a_ref, w_ref, *refs, scale):
    outs, wbf = refs[:-1], refs[-1]

    @pl.when(pl.program_id(1) == 0)
    def _():
        wbf[...] = w_ref[...].astype(BF16)

    acc = jnp.dot(a_ref[...], wbf[...], preferred_element_type=F32)
    if scale != 1.0:
        acc = acc * scale
    for o in outs:
        o[...] = acc.astype(o.dtype)


def _proj(a, w, col_off, n_cols, out_dtypes, tm, tn, scale=1.0, name="proj"):
    m, k = a.shape
    off = col_off // tn
    out_spec = pl.BlockSpec((tm, tn), lambda j, i: (i, j))
    outs = pl.pallas_call(
        functools.partial(_proj_kernel, scale=scale),
        grid=(n_cols // tn, m // tm),
        in_specs=[pl.BlockSpec((tm, k), lambda j, i: (i, 0)),
                  pl.BlockSpec((k, tn), lambda j, i: (0, j + off))],
        out_specs=[out_spec] * len(out_dtypes),
        out_shape=[jax.ShapeDtypeStruct((m, n_cols), dt) for dt in out_dtypes],
        scratch_shapes=[pltpu.VMEM((k, tn), BF16)],
        compiler_params=_cparams("arbitrary", "arbitrary"),
        name=name,
    )(a, w)
    return outs


def _mix_kernel(a1_ref, a2_ref, w1_ref, w2_ref, ga_ref, gb_ref, o_ref, w1bf, w2bf):
    @pl.when(pl.program_id(1) == 0)
    def _():
        w1bf[...] = w1_ref[...].astype(BF16)
        w2bf[...] = w2_ref[...].astype(BF16)

    y1 = jnp.dot(a1_ref[...], w1bf[...], preferred_element_type=F32)
    y2 = jnp.dot(a2_ref[...], w2bf[...], preferred_element_type=F32)
    o_ref[...] = (jax.nn.sigmoid(ga_ref[...]) * y1 + jax.nn.sigmoid(gb_ref[...]) * y2).astype(o_ref.dtype)


def _mix(a1, a2, w1, w2, gates, tm, tn):
    m, k = a1.shape
    n = w1.shape[1]
    nb = n // tn
    return pl.pallas_call(
        _mix_kernel,
        grid=(nb, m // tm),
        in_specs=[pl.BlockSpec((tm, k), lambda j, i: (i, 0)),
                  pl.BlockSpec((tm, k), lambda j, i: (i, 0)),
                  pl.BlockSpec((k, tn), lambda j, i: (0, j)),
                  pl.BlockSpec((k, tn), lambda j, i: (0, j)),
                  pl.BlockSpec((tm, tn), lambda j, i: (i, j)),
                  pl.BlockSpec((tm, tn), lambda j, i: (i, j + nb))],
        out_specs=pl.BlockSpec((tm, tn), lambda j, i: (i, j)),
        out_shape=jax.ShapeDtypeStruct((m, n), BF16),
        scratch_shapes=[pltpu.VMEM((k, tn), BF16), pltpu.VMEM((k, tn), BF16)],
        compiler_params=_cparams("arbitrary", "arbitrary"),
        name="branch_mix",
    )(a1, a2, w1, w2, gates, gates)


def _resid_kernel(a_ref, w_ref, x_ref, g_ref, o_ref, wbf):
    @pl.when(pl.program_id(1) == 0)
    def _():
        wbf[...] = w_ref[...].astype(BF16)

    y = jnp.dot(a_ref[...], wbf[...], preferred_element_type=F32)
    o_ref[...] = x_ref[...] + g_ref[...] * y


def _resid(a, w, x, g, tm, tn, tiles_per_group):
    m, k = a.shape
    n = w.shape[1]
    rm = g.shape[1]
    return pl.pallas_call(
        _resid_kernel,
        grid=(n // tn, m // tm),
        in_specs=[pl.BlockSpec((tm, k), lambda j, i: (i, 0)),
                  pl.BlockSpec((k, tn), lambda j, i: (0, j)),
                  pl.BlockSpec((tm, tn), lambda j, i: (i, j)),
                  pl.BlockSpec((None, rm, tn), lambda j, i: (i // tiles_per_group, 0, j))],
        out_specs=pl.BlockSpec((tm, tn), lambda j, i: (i, j)),
        out_shape=jax.ShapeDtypeStruct((m, n), F32),
        scratch_shapes=[pltpu.VMEM((k, tn), BF16)],
        compiler_params=_cparams("arbitrary", "arbitrary"),
        name="out_proj_residual",
    )(a, w, x, g)


def _pattn_kernel(rb_ref, lam_ref, gain_ref, q_ref, k_ref, vt_ref, o_ref, bias_s, *, blk, lam_init, n_buckets, nq):
    h = pl.program_id(1)
    qi = pl.program_id(2)
    dk = q_ref.shape[1] // 2

    @pl.when(qi == 0)
    def _():
        r = lax.broadcasted_iota(I32, (blk, blk), 0)
        c = lax.broadcasted_iota(I32, (blk, blk), 1)
        far = rb_ref[h, n_buckets - 1]
        for which in range(2):
            dist = c - r + which * blk
            bkt = _t5_bucket(dist, n_buckets)
            val = jnp.zeros((blk, blk), F32)
            for b in range(n_buckets):
                val = jnp.where(bkt == b, rb_ref[h, b], val)
            val = val - far
            if which == 0:
                val = jnp.where(dist >= 0, val, MASKED)
            bias_s[which] = val

    q = q_ref[...]
    lane = lax.broadcasted_iota(I32, q.shape, 1)
    qmaps = (jnp.where(lane < dk, q, jnp.zeros_like(q)), jnp.where(lane >= dk, q, jnp.zeros_like(q)))

    lam = _diff_lambda(lam_ref, lam_init)

    def attend(n):
        ks = [k_ref[j * blk:(j + 1) * blk, :] for j in range(n)]
        vts = [vt_ref[j] for j in range(n)]
        biases = [None] * (n - 2) + [bias_s[1], bias_s[0]][-min(n, 2):]
        outs = []
        for mp in range(2):
            scores = []
            for k, bias in zip(ks, biases):
                s = lax.dot_general(k, qmaps[mp], NT_DIMS, preferred_element_type=F32)
                scores.append(s if bias is None else s + bias)
            smax = scores[0]
            for s in scores[1:]:
                smax = jnp.maximum(smax, s)
            m = jnp.max(smax, axis=0, keepdims=True)
            probs = [jnp.exp(s - m) for s in scores]
            psum = probs[0]
            for p in probs[1:]:
                psum = psum + p
            pv = None
            for p, vt in zip(probs, vts):
                d = jnp.dot(vt, p.astype(BF16), preferred_element_type=F32)
                pv = d if pv is None else pv + d
            outs.append(pv / jnp.sum(psum, axis=0, keepdims=True))
        ot = outs[0] - lam * outs[1]
        o_ref[...] = (_rms(ot.T, gain_ref[...]) * (1.0 - lam_init)).astype(o_ref.dtype)

    for n in range(1, nq + 1):
        pl.when(qi == n - 1)(functools.partial(attend, n))


def _prompt_attention(q, k, vt, rel_bias_t, lam_vecs, gain, batch, seq, lam_init, blk):
    n_heads, n_buckets = rel_bias_t.shape
    dv = vt.shape[2]
    assert blk >= MAX_DISTANCE and seq % blk == 0
    nq = seq // blk
    return pl.pallas_call(
        functools.partial(_pattn_kernel, blk=blk, lam_init=lam_init, n_buckets=n_buckets, nq=nq),
        grid=(batch, n_heads, nq),
        in_specs=[pl.BlockSpec(memory_space=pltpu.SMEM),
                  pl.BlockSpec(lam_vecs.shape, lambda b, h, i: (0, 0)),
                  pl.BlockSpec((1, dv), lambda b, h, i: (0, 0)),
                  pl.BlockSpec((blk, dv), lambda b, h, i: (b * nq + i, h)),
                  pl.BlockSpec((seq, dv), lambda b, h, i: (b, h)),
                  pl.BlockSpec((None, nq, dv, blk), lambda b, h, i: (b * n_heads + h, 0, 0, 0))],
        out_specs=pl.BlockSpec((blk, dv), lambda b, h, i: (b * nq + i, h)),
        out_shape=jax.ShapeDtypeStruct(q.shape, BF16),
        scratch_shapes=[pltpu.VMEM((2, blk, blk), F32)],
        compiler_params=_cparams("arbitrary", "arbitrary", "arbitrary"),
        name="prompt_diff_attention",
    )(rel_bias_t, lam_vecs, gain.reshape(1, dv), q, k, vt)


def _sattn_kernel(pt_ref, rbt_ref, lam_ref, gain_ref, qb_ref, ks_ref, vs_ref, ck_hbm, cv_hbm, o_ref,
                  kbuf, vbuf, mb_s, mbs_s, m_s, l_s, acc_s, sem,
                  *, group, n_pages, n_batch, page_off, lam_init, n_buckets, n_heads, t_new):
    b = pl.program_id(0)
    n_groups = n_pages // group
    rows = qb_ref.shape[0]
    half = rows // 2
    cols = kbuf.shape[2]
    scols = ks_ref.shape[0]

    @pl.when(b == 0)
    def _():
        def lookup(dist, width):
            bkt = _t5_bucket(dist, n_buckets)
            val = jnp.zeros(dist.shape, F32)
            for i in range(n_buckets):
                val = jnp.where(bkt == i, rbt_ref[i:i + 1, 0:width], val)
            return val - rbt_ref[n_buckets - 1:n_buckets, 0:width]

        r = lax.broadcasted_iota(I32, (rows, cols), 0)
        c = lax.broadcasted_iota(I32, (rows, cols), 1)
        hrow = (r % half) // t_new
        trow = r % t_new
        valid = hrow == c % n_heads
        mb_s[0] = jnp.where(valid, 0.0, MASKED)
        mb_s[1] = jnp.where(valid, lookup(PAGE_SIZE + trow - c // n_heads, cols), MASKED)
        r2 = lax.broadcasted_iota(I32, (rows, scols), 0)
        c2 = lax.broadcasted_iota(I32, (rows, scols), 1)
        t2 = r2 % t_new
        key2 = c2 // n_heads
        valid2 = ((r2 % half) // t_new == c2 % n_heads) & (key2 <= t2)
        mbs_s[...] = jnp.where(valid2, lookup(t2 - key2, scols), MASKED)

    def copies(bb, g, slot):
        out = []
        for j in range(group):
            page = pt_ref[bb, g * group + j] + page_off
            out.append(pltpu.make_async_copy(ck_hbm.at[page], kbuf.at[slot, j], sem.at[slot, 0, j]))
            out.append(pltpu.make_async_copy(cv_hbm.at[page], vbuf.at[slot, j], sem.at[slot, 1, j]))
        return out

    @pl.when(b == 0)
    def _():
        for cp in copies(0, 0, 0):
            cp.start()

    m_s[...] = jnp.full(m_s.shape, -jnp.inf, F32)
    l_s[...] = jnp.zeros(l_s.shape, F32)
    acc_s[...] = jnp.zeros(acc_s.shape, F32)
    qb = qb_ref[...]

    def update(blocks):
        scores = [lax.dot_general(qb, kp, NT_DIMS, preferred_element_type=F32) + mb for kp, _, mb in blocks]
        smax = scores[0]
        for s in scores[1:]:
            smax = jnp.maximum(smax, s)
        m_old = m_s[...]
        m_new = jnp.maximum(m_old, jnp.max(smax, axis=-1, keepdims=True))
        corr = jnp.exp(m_old - m_new)
        probs = [jnp.exp(s - m_new) for s in scores]
        psum = probs[0]
        for p in probs[1:]:
            psum = psum + p
        pv = None
        for p, (_, vp, _) in zip(probs, blocks):
            d = jnp.dot(p.astype(BF16), vp, preferred_element_type=F32)
            pv = d if pv is None else pv + d
        l_s[...] = l_s[...] * corr + jnp.sum(psum, axis=-1, keepdims=True)
        acc_s[...] = acc_s[...] * corr + pv
        m_s[...] = m_new

    @pl.loop(0, n_groups)
    def _(g):
        slot = g % 2
        nxt = g + 1

        @pl.when(nxt < n_groups)
        def _():
            for cp in copies(b, nxt, 1 - slot):
                cp.start()

        @pl.when((nxt == n_groups) & (b + 1 < n_batch))
        def _():
            for cp in copies(b + 1, 0, 1 - slot):
                cp.start()

        for cp in copies(b, g, slot):
            cp.wait()
        last = mb_s[jnp.where(g == n_groups - 1, 1, 0)]
        update([(kbuf[slot, j].astype(BF16), vbuf[slot, j].astype(BF16), last if j == group - 1 else mb_s[0])
                for j in range(group)])

    update([(ks_ref[...], vs_ref[...], mbs_s[...])])

    lam = _diff_lambda(lam_ref, lam_init)
    o = acc_s[...] / l_s[...]
    od = o[0:half] - lam * o[half:rows]
    o_ref[...] = (_rms(od, gain_ref[...]) * (1.0 - lam_init)).astype(o_ref.dtype)


def _sample_attention(qb, ks, vs, cache_k2, cache_v2, page_table, page_off, rbt, lam_vecs, gain, lam_init,
                      n_heads, t_new, group=8):
    n_batch, rows, dv = qb.shape
    n_pages = page_table.shape[1]
    cols = cache_k2.shape[1]
    n_buckets = rbt.shape[0]
    assert n_pages % group == 0 and (n_pages // group) % 2 == 0
    kern = functools.partial(_sattn_kernel, group=group, n_pages=n_pages, n_batch=n_batch, page_off=page_off,
                             lam_init=lam_init, n_buckets=n_buckets, n_heads=n_heads, t_new=t_new)
    grid_spec = pltpu.PrefetchScalarGridSpec(
        num_scalar_prefetch=1,
        grid=(n_batch,),
        in_specs=[pl.BlockSpec(rbt.shape, lambda b, pt: (0, 0)),
                  pl.BlockSpec(lam_vecs.shape, lambda b, pt: (0, 0)),
                  pl.BlockSpec((1, dv), lambda b, pt: (0, 0)),
                  pl.BlockSpec((None, rows, dv), lambda b, pt: (b, 0, 0)),
                  pl.BlockSpec((None,) + ks.shape[1:], lambda b, pt: (b, 0, 0)),
                  pl.BlockSpec((None,) + vs.shape[1:], lambda b, pt: (b, 0, 0)),
                  pl.BlockSpec(memory_space=pl.ANY),
                  pl.BlockSpec(memory_space=pl.ANY)],
        out_specs=pl.BlockSpec((None, rows // 2, dv), lambda b, pt: (b, 0, 0)),
        scratch_shapes=[pltpu.VMEM((2, group, cols, dv), F32), pltpu.VMEM((2, group, cols, dv), F32),
                        pltpu.VMEM((2, rows, cols), F32), pltpu.VMEM((rows, ks.shape[1]), F32),
                        pltpu.VMEM((rows, 1), F32), pltpu.VMEM((rows, 1), F32), pltpu.VMEM((rows, dv), F32),
                        pltpu.SemaphoreType.DMA((2, 2, group))])
    return pl.pallas_call(
        kern, grid_spec=grid_spec,
        out_shape=jax.ShapeDtypeStruct((n_batch, rows // 2, dv), BF16),
        compiler_params=_cparams("arbitrary"),
        name="sample_diff_attention",
    )(page_table, rbt, lam_vecs, gain.reshape(1, dv), qb, ks, vs, cache_k2, cache_v2)


def _rg_gates(xc, wrg_ref, brg_ref, wig_ref, big_ref, lam_ref):
    n_blocks, blk = wrg_ref.shape[0], wrg_ref.shape[1]
    rs, gs = [], []
    for n in range(n_blocks):
        xb = xc[:, n * blk:(n + 1) * blk].astype(BF16)
        rs.append(jnp.dot(xb, wrg_ref[n].astype(BF16), preferred_element_type=F32))
        gs.append(jnp.dot(xb, wig_ref[n].astype(BF16), preferred_element_type=F32))
    r = jax.nn.sigmoid(jnp.concatenate(rs, axis=-1) + brg_ref[...])
    i = jax.nn.sigmoid(jnp.concatenate(gs, axis=-1) + big_ref[...])
    log_a = (-RG_C * _softplus(-lam_ref[...])) * r
    a = jnp.exp(log_a)
    th = jnp.tanh(log_a)
    u = xc * i * jnp.sqrt(-2.0 * th / (1.0 - th))
    return a, u


def _rglru_prompt_kernel(x_ref, g_ref, wc_ref, bc_ref, wrg_ref, brg_ref, wig_ref, big_ref, lam_ref,
                         hr_ref, hl_ref, xbuf, a_s, u_s, h_s, *, conv_w):
    i = pl.program_id(1)
    tc = x_ref.shape[0]
    d = x_ref.shape[1]

    @pl.when(i == 0)
    def _():
        xbuf[0:SUBLANES] = jnp.zeros((SUBLANES, d), F32)
        h_s[...] = jnp.zeros(h_s.shape, F32)

    xbuf[SUBLANES:SUBLANES + tc] = x_ref[...]
    xc = bc_ref[...]
    for j in range(conv_w):
        lo = SUBLANES - (conv_w - 1) + j
        xc = xc + wc_ref[j:j + 1] * xbuf[lo:lo + tc]
    xbuf[0:SUBLANES] = xbuf[tc:tc + SUBLANES]

    a, u = _rg_gates(xc, wrg_ref, brg_ref, wig_ref, big_ref, lam_ref)
    a_s[...] = a
    u_s[...] = u
    row = lax.broadcasted_iota(I32, (SUBLANES, d), 0)

    @pl.loop(0, tc // SUBLANES)
    def _(g):
        st = pl.multiple_of(g * SUBLANES, SUBLANES)
        a8 = a_s[pl.ds(st, SUBLANES), :]
        u8 = u_s[pl.ds(st, SUBLANES), :]
        for s in (1, 2, 4):
            a_prev = jnp.where(row >= s, pltpu.roll(a8, s, 0), 1.0)
            u_prev = jnp.where(row >= s, pltpu.roll(u8, s, 0), 0.0)
            u8 = a8 * u_prev + u8
            a8 = a8 * a_prev
        h8 = a8 * h_s[...] + u8
        h_s[...] = jnp.broadcast_to(h8[SUBLANES - 1:SUBLANES], (SUBLANES, d))
        u_s[pl.ds(st, SUBLANES), :] = h8

    hr_ref[...] = (u_s[...] * _gelu_tanh(g_ref[...])).astype(hr_ref.dtype)

    @pl.when(i == pl.num_programs(1) - 1)
    def _():
        hl_ref[...] = h_s[0:1]


def _rglru_prompt(rg, w_conv, b_conv, w_rg, b_rg, w_ig, b_ig, lam_rg, batch, seq, tc=256):
    d = rg.shape[1] // 2
    nt = seq // tc
    conv_w = w_conv.shape[0]
    assert conv_w - 1 <= SUBLANES and seq % tc == 0
    vec = pl.BlockSpec((1, d), lambda b, i: (0, 0))
    wblk = pl.BlockSpec(w_rg.shape, lambda b, i: (0, 0, 0))
    return pl.pallas_call(
        functools.partial(_rglru_prompt_kernel, conv_w=conv_w),
        grid=(batch, nt),
        in_specs=[pl.BlockSpec((tc, d), lambda b, i: (b * nt + i, 0)),
                  pl.BlockSpec((tc, d), lambda b, i: (b * nt + i, 1)),
                  pl.BlockSpec((conv_w, d), lambda b, i: (0, 0)), vec, wblk, vec, wblk, vec, vec],
        out_specs=[pl.BlockSpec((tc, d), lambda b, i: (b * nt + i, 0)),
                   pl.BlockSpec((None, 1, d), lambda b, i: (b, 0, 0))],
        out_shape=[jax.ShapeDtypeStruct((batch * seq, d), BF16), jax.ShapeDtypeStruct((batch, 1, d), F32)],
        scratch_shapes=[pltpu.VMEM((tc + SUBLANES, d), F32), pltpu.VMEM((tc, d), F32),
                        pltpu.VMEM((tc, d), F32), pltpu.VMEM((SUBLANES, d), F32)],
        compiler_params=_cparams("arbitrary", "arbitrary"),
        name="rglru_prompt",
    )(rg, rg, w_conv, b_conv.reshape(1, d), w_rg, b_rg.reshape(1, d), w_ig, b_ig.reshape(1, d),
      lam_rg.reshape(1, d))


def _rglru_sample_kernel(x_ref, g_ref, cs_ref, h0_ref, wc_ref, bc_ref, wrg_ref, brg_ref, wig_ref, big_ref,
                         lam_ref, hr_ref, hl_ref, *, conv_w):
    t_new = x_ref.shape[0]
    buf = [cs_ref[j] for j in range(conv_w - 1)] + [x_ref[t] for t in range(t_new)]
    h = h0_ref[...]
    for t in range(t_new):
        xc = bc_ref[...]
        for j in range(conv_w):
            xc = xc + wc_ref[j:j + 1] * buf[t + j]
        a, u = _rg_gates(xc, wrg_ref, brg_ref, wig_ref, big_ref, lam_ref)
        h = a * h + u
        hr_ref[t] = (h * _gelu_tanh(g_ref[t])).astype(hr_ref.dtype)
    hl_ref[...] = h


def _rglru_sample(x_t, g_t, conv_t, h0, w_conv, b_conv, w_rg, b_rg, w_ig, b_ig, lam_rg):
    t_new, batch, d = x_t.shape
    return pl.pallas_call(
        functools.partial(_rglru_sample_kernel, conv_w=w_conv.shape[0]),
        out_shape=[jax.ShapeDtypeStruct((t_new, batch, d), BF16), jax.ShapeDtypeStruct((batch, d), F32)],
        compiler_params=pltpu.CompilerParams(vmem_limit_bytes=V7X_VMEM_LIMIT_BYTES),
        name="rglru_sample",
    )(x_t, g_t, conv_t, h0, w_conv, b_conv.reshape(1, d), w_rg, b_rg.reshape(1, d), w_ig, b_ig.reshape(1, d),
      lam_rg.reshape(1, d))


def _router_kernel(xp_ref, xs_ref, gain_ref, sc_ref, sh_ref, wr_ref, br_ref, hn_ref, idx_ref, gate_ref, rank_ref,
                   cnt_ref, base_s, *, n_prompt_tiles):
    i = pl.program_id(0)
    tm = xp_ref.shape[0]
    n_exp = wr_ref.shape[1]

    @pl.when(i == 0)
    def _():
        base_s[...] = jnp.zeros(base_s.shape, F32)

    x = jnp.where(i < n_prompt_tiles, xp_ref[...], xs_ref[...])
    hn = _rms(x, gain_ref[...]) * (1.0 + sc_ref[...]) + sh_ref[...]
    hb = hn.astype(BF16)
    bits = lax.bitcast_convert_type(hb.astype(F32), U32)
    half = bits.shape[1] // 2
    hn_ref[...] = (bits[:, half:] & HIGH_HALF) | (bits[:, :half] >> 16)
    logits = jnp.dot(hb, wr_ref[...].astype(BF16), preferred_element_type=F32) + br_ref[...]

    lane = lax.broadcasted_iota(I32, (tm, n_exp), 1)
    work = logits
    sels, vals = [], []
    for _ in range(TOP_K):
        mx = jnp.max(work, axis=-1, keepdims=True)
        first = jnp.min(jnp.where(work == mx, lane, n_exp), axis=-1, keepdims=True)
        sel = lane == first
        sels.append(sel)
        vals.append(mx)
        idx_ref[:, len(sels) - 1:len(sels)] = first
        work = jnp.where(sel, -jnp.inf, work)

    es = [jnp.exp(v - vals[0]) for v in vals]
    tot = es[0]
    for e in es[1:]:
        tot = tot + e
    for k in range(TOP_K):
        gate_ref[:, k:k + 1] = es[k] / tot

    chosen = sels[0]
    for sel in sels[1:]:
        chosen = chosen | sel
    onehot = jnp.where(chosen, 1.0, 0.0)
    r = lax.broadcasted_iota(I32, (tm, tm), 0)
    c = lax.broadcasted_iota(I32, (tm, tm), 1)
    tri = jnp.where(r > c, 1.0, 0.0).astype(BF16)
    pos = jnp.dot(tri, onehot.astype(BF16), preferred_element_type=F32) + base_s[...]
    for k in range(TOP_K):
        rank_ref[:, k:k + 1] = jnp.sum(jnp.where(sels[k], pos, 0.0), axis=-1, keepdims=True).astype(I32)
    base_s[...] = base_s[...] + jnp.sum(onehot, axis=0, keepdims=True)
    cnt_ref[...] = base_s[...].astype(I32)


def _router(x1p, x1s, gain, sc, sh, w_router, b_router, tiles_per_group):
    tm, d = x1s.shape
    npt = x1p.shape[0] // tm
    r = x1p.shape[0] + tm
    n_exp = w_router.shape[1]
    n_groups = sc.shape[0]
    mod_spec = pl.BlockSpec((None, tm, d), lambda i: (jnp.minimum(i // tiles_per_group, n_groups - 1), 0, 0))
    small = pl.BlockSpec((tm, TOP_K), lambda i: (i, 0))
    return pl.pallas_call(
        functools.partial(_router_kernel, n_prompt_tiles=npt),
        grid=(r // tm,),
        in_specs=[pl.BlockSpec((tm, d), lambda i: (jnp.minimum(i, npt - 1), 0)),
                  pl.BlockSpec((tm, d), lambda i: (0, 0)), pl.BlockSpec((1, d), lambda i: (0, 0)),
                  mod_spec, mod_spec,
                  pl.BlockSpec((d, n_exp), lambda i: (0, 0)), pl.BlockSpec((1, n_exp), lambda i: (0, 0))],
        out_specs=[pl.BlockSpec((tm, d // 2), lambda i: (i, 0)), small, small, small,
                   pl.BlockSpec((1, n_exp), lambda i: (0, 0))],
        out_shape=[jax.ShapeDtypeStruct((r, d // 2), U32), jax.ShapeDtypeStruct((r, TOP_K), I32),
                   jax.ShapeDtypeStruct((r, TOP_K), F32), jax.ShapeDtypeStruct((r, TOP_K), I32),
                   jax.ShapeDtypeStruct((1, n_exp), I32)],
        scratch_shapes=[pltpu.VMEM((1, n_exp), F32)],
        compiler_params=_cparams("arbitrary"),
        name="router_topk",
    )(x1p, x1s, gain.reshape(1, d), sc, sh, w_router, b_router.reshape(1, n_exp))


@functools.lru_cache(maxsize=None)
def _row_block_plan(units, cap, sizes, unit_rows):
    def cost(sz):
        return max(EXPERT_BLOCK_MIN_COST, EXPERT_BLOCK_ROW_COST * sz * unit_rows)

    @functools.lru_cache(maxsize=None)
    def solve(r, room):
        if r <= 0:
            return 0.0, ()
        best = None
        for sz in sizes:
            if sz <= room:
                c, p = solve(r - sz, room - sz)
                if best is None or cost(sz) + c < best[0]:
                    best = (cost(sz) + c, (sz,) + p)
        return best

    return solve(units, cap)[1]


def _expert_kernel(te_ref, rows_ref, nu_ref, src_ref, hn_hbm, wg_ref, wu_ref, bg_ref, bu_ref, wd_ref, bd_ref, o_ref,
                   xg, xb, wgb, wub, wdb, sem, *, big, chunk):
    t = pl.program_id(0)
    f = pl.program_id(1)
    tb = xb.shape[0]
    n_used = nu_ref[0]

    def n_chunks(tile):
        return (rows_ref[tile] + chunk - 1) // chunk

    def gather(tile):
        base = tile * tb

        @pl.loop(0, n_chunks(tile) * (chunk // GATHER_UNROLL))
        def _(i):
            r0 = i * GATHER_UNROLL
            for j in range(GATHER_UNROLL):
                tok = src_ref[base + r0 + j]
                pltpu.make_async_copy(hn_hbm.at[pl.ds(tok, 1)], xg.at[pl.ds(r0 + j, 1)], sem.at[0]).start()

    @pl.when((t == 0) & (f == 0))
    def _():
        xb[...] = jnp.zeros(xb.shape, BF16)
        gather(0)

    @pl.when((t < n_used) & (f == 0))
    def _():
        @pl.loop(0, n_chunks(t))
        def _(c):
            pltpu.make_async_copy(hn_hbm.at[pl.ds(0, chunk)], xg.at[pl.ds(0, chunk)], sem.at[0]).wait()

        half = xg.shape[1]

        @pl.loop(0, n_chunks(t))
        def _(c):
            rs = pl.ds(pl.multiple_of(c * chunk, chunk), chunk)
            w = xg[rs, :]
            xb[rs, 0:half] = lax.bitcast_convert_type(w << 16, F32).astype(BF16)
            xb[rs, half:2 * half] = lax.bitcast_convert_type(w & HIGH_HALF, F32).astype(BF16)

        o_ref[...] = jnp.broadcast_to(bd_ref[...], o_ref.shape)

        @pl.when(t + 1 < n_used)
        def _():
            gather(t + 1)

    @pl.when(t < n_used)
    def _():
        wgb[...] = wg_ref[...].astype(BF16)
        wub[...] = wu_ref[...].astype(BF16)
        wdb[...] = wd_ref[...].astype(BF16)

        def ffn(start, size):
            rs = pl.ds(pl.multiple_of(start, chunk), size)
            x = xb[rs, :]
            g = jnp.dot(x, wgb[...], preferred_element_type=F32) + bg_ref[...]
            u = jnp.dot(x, wub[...], preferred_element_type=F32) + bu_ref[...]
            g = jnp.minimum(g, SWIGLU_LIMIT)
            u = jnp.clip(u, -SWIGLU_LIMIT, SWIGLU_LIMIT)
            act = g * jax.nn.sigmoid(SWIGLU_ALPHA * g) * (u + 1.0)
            o_ref[rs, :] += jnp.dot(act.astype(BF16), wdb[...], preferred_element_type=F32)

        units = (rows_ref[t] + chunk - 1) // chunk
        big_u, tb_u = big // chunk, tb // chunk
        n_big = units // big_u
        rem = units - n_big * big_u
        sizes = []
        size = big
        while size >= chunk:
            sizes.append(size)
            size //= 2
        counts = {size: jnp.where(size == big, n_big, 0) for size in sizes}
        for nb in range(tb_u // big_u + 1):
            cap = min(big_u, tb_u - nb * big_u)
            for r in range(1, cap + 1):
                plan = _row_block_plan(r, cap, tuple(sz // chunk for sz in sizes), chunk)
                for size in sizes:
                    extra = plan.count(size // chunk)
                    if extra:
                        hit = (n_big == nb) & (rem == r)
                        counts[size] = jnp.where(hit, (nb if size == big else 0) + extra, counts[size])
        done = 0
        for size in sizes:
            @pl.loop(0, counts[size])
            def _(i, done=done, size=size):
                ffn(done + i * size, size)

            done = done + counts[size] * size

    @pl.when((t >= n_used) & (f == 0))
    def _():
        o_ref[...] = jnp.zeros(o_ref.shape, F32)


def _experts(hn, tile_expert, tile_rows, n_used, src_token, w_up, b_up, w_down, b_down, tb, tf, big, chunk):
    n_exp, d, ff2 = w_up.shape
    ff = ff2 // 2
    nf = ff // tf
    n_tiles = tile_expert.shape[0]
    assert tb % chunk == 0 and big % chunk == 0 and chunk % GATHER_UNROLL == 0

    def eff(t, f, nu):
        return jnp.minimum(t, nu[0] - 1), jnp.where(t < nu[0], f, nf - 1)

    def wg_map(t, f, te, rows, nu, src):
        tt, fe = eff(t, f, nu)
        return te[tt], 0, fe

    def wu_map(t, f, te, rows, nu, src):
        tt, fe = eff(t, f, nu)
        return te[tt], 0, fe + nf

    def wd_map(t, f, te, rows, nu, src):
        tt, fe = eff(t, f, nu)
        return te[tt], fe, 0

    def bd_map(t, f, te, rows, nu, src):
        tt, _ = eff(t, f, nu)
        return te[tt], 0, 0

    def out_map(t, f, te, rows, nu, src):
        return t, 0

    grid_spec = pltpu.PrefetchScalarGridSpec(
        num_scalar_prefetch=4,
        grid=(n_tiles, nf),
        in_specs=[pl.BlockSpec(memory_space=pl.ANY),
                  pl.BlockSpec((None, d, tf), wg_map), pl.BlockSpec((None, d, tf), wu_map),
                  pl.BlockSpec((None, 1, tf), wg_map), pl.BlockSpec((None, 1, tf), wu_map),
                  pl.BlockSpec((None, tf, d), wd_map), pl.BlockSpec((None, 1, d), bd_map)],
        out_specs=pl.BlockSpec((tb, d), out_map),
        scratch_shapes=[pltpu.VMEM((tb, d // 2), U32), pltpu.VMEM((tb, d), BF16),
                        pltpu.VMEM((d, tf), BF16), pltpu.VMEM((d, tf), BF16), pltpu.VMEM((tf, d), BF16),
                        pltpu.SemaphoreType.DMA((1,))])
    return pl.pallas_call(
        functools.partial(_expert_kernel, big=big, chunk=chunk), grid_spec=grid_spec,
        out_shape=jax.ShapeDtypeStruct((n_tiles * tb, d), F32),
        compiler_params=_cparams("arbitrary", "arbitrary"),
        name="routed_experts",
    )(tile_expert, tile_rows, n_used, src_token, hn, w_up, w_up, b_up.reshape(n_exp, 1, ff2),
      b_up.reshape(n_exp, 1, ff2), w_down, b_down.reshape(n_exp, 1, d))


def _combine_kernel(dest_ref, ys_hbm, xp_ref, xs_ref, gate_ref, g2_ref, fn_ref, yp_ref, ysm_ref, buf, sem,
                    *, n_prompt_tiles):
    i = pl.program_id(0)
    n = pl.num_programs(0)
    tm = xp_ref.shape[0]
    slot = i % 2

    def gather(tile, into):
        base = tile * tm * TOP_K
        per_iter = GATHER_UNROLL // TOP_K

        @pl.loop(0, tm // per_iter)
        def _(it):
            for j in range(per_iter):
                r = it * per_iter + j
                for k in range(TOP_K):
                    row = dest_ref[base + r * TOP_K + k]
                    pltpu.make_async_copy(ys_hbm.at[pl.ds(row, 1)], buf.at[into, k, pl.ds(r, 1)],
                                          sem.at[into]).start()

    @pl.when(i == 0)
    def _():
        gather(0, 0)

    @pl.when(i + 1 < n)
    def _():
        gather(i + 1, 1 - slot)

    for k in range(TOP_K):
        pltpu.make_async_copy(ys_hbm.at[pl.ds(0, tm)], buf.at[slot, k], sem.at[slot]).wait()

    moe = gate_ref[:, 0:1] * buf[slot, 0]
    for k in range(1, TOP_K):
        moe = moe + gate_ref[:, k:k + 1] * buf[slot, k]
    x = jnp.where(i < n_prompt_tiles, xp_ref[...], xs_ref[...])
    y = _rms(x + g2_ref[...] * moe, fn_ref[...])

    @pl.when(i < n_prompt_tiles)
    def _():
        yp_ref[...] = y

    @pl.when(i >= n_prompt_tiles)
    def _():
        ysm_ref[...] = y


def _combine(dest_flat, ys, x1p, x1s, gate, g2, final_norm, tiles_per_group):
    tm, d = x1s.shape
    n_prompt_rows = x1p.shape[0]
    r = n_prompt_rows + tm
    n_groups = g2.shape[0]
    npt = n_prompt_rows // tm
    grid_spec = pltpu.PrefetchScalarGridSpec(
        num_scalar_prefetch=1,
        grid=(r // tm,),
        in_specs=[pl.BlockSpec(memory_space=pl.ANY),
                  pl.BlockSpec((tm, d), lambda i, dst: (jnp.minimum(i, npt - 1), 0)),
                  pl.BlockSpec((tm, d), lambda i, dst: (0, 0)),
                  pl.BlockSpec((tm, TOP_K), lambda i, dst: (i, 0)),
                  pl.BlockSpec((None, tm, d), lambda i, dst: (jnp.minimum(i // tiles_per_group, n_groups - 1), 0, 0)),
                  pl.BlockSpec((1, d), lambda i, dst: (0, 0))],
        out_specs=[pl.BlockSpec((tm, d), lambda i, dst: (jnp.minimum(i, npt - 1), 0)),
                   pl.BlockSpec((tm, d), lambda i, dst: (0, 0))],
        scratch_shapes=[pltpu.VMEM((2, TOP_K, tm, d), F32), pltpu.SemaphoreType.DMA((2,))])
    return pl.pallas_call(
        functools.partial(_combine_kernel, n_prompt_tiles=npt), grid_spec=grid_spec,
        out_shape=[jax.ShapeDtypeStruct((n_prompt_rows, d), F32), jax.ShapeDtypeStruct((tm, d), F32)],
        compiler_params=_cparams("arbitrary"),
        name="expert_combine_final_norm",
    )(dest_flat, ys, x1p, x1s, gate, g2, final_norm.reshape(1, d))


def kernel(x_prompt, x_sample, c_prompt, c_sample, cache_k, cache_v, state_conv, state_h, page_table, rel_bias, w_ada, b_ada, norm_mix, norm_ffn, w_in, lam_q1, lam_k1, lam_q2, lam_k2, subln_gain, w_conv, b_conv, w_rg, b_rg, w_ig, b_ig, lam_rg, w_att_out, w_rnn_out, w_o, w_router, b_router, w_up, b_up, w_down, b_down, final_norm):
    bp, seq, d = x_prompt.shape
    bs, t_new, _ = x_sample.shape
    depth, n_pool, page, n_heads, dv = cache_v.shape
    assert page == PAGE_SIZE and cache_k.shape[-1] == dv
    d_rnn = w_conv.shape[-1]
    qk_w = n_heads * dv
    n_exp = w_router.shape[-1]
    rp, rs = bp * seq, bs * t_new
    scale = (dv // 2) ** -0.5
    assert depth == 1
    tm_p, tm_s = PROMPT_ROW_TILE, rs
    tok_tile = rs
    assert seq % tm_p == 0 and rp % tok_tile == 0 and rs % SUBLANES == 0
    expert_rows, ff_tile, tn = EXPERT_ROW_TILE, FF_TILE, WEIGHT_COL_TILE

    xp = x_prompt.reshape(rp, d)
    xs = x_sample.reshape(rs, d)
    c_all = jnp.concatenate([c_prompt, c_sample, jnp.zeros((-(bp + bs) % SUBLANES, d), F32)], axis=0)
    cache_k2 = cache_k.reshape(depth * n_pool, page * n_heads, dv)
    cache_v2 = cache_v.reshape(depth * n_pool, page * n_heads, dv)
    rel_bias_t = rel_bias.T
    rbt = jnp.tile(rel_bias, (1, page))

    outs = {k: [] for k in ("kp", "vp", "cp", "hp", "ks", "vs", "cs", "hs")}
    for l in range(depth):
        lam_init = 0.8 - 0.6 * math.exp(-0.3 * l)
        lam_vecs = jnp.stack([lam_q1[l], lam_k1[l], lam_q2[l], lam_k2[l]])
        mod = _ada(c_all, w_ada[l], b_ada[l])
        mods_p = [m.reshape(bp, 1, d) for m in jnp.split(mod[:bp], 6, axis=-1)]
        mods_s = [jnp.repeat(m, t_new, axis=0).reshape(1, rs, d) for m in jnp.split(mod[bp:bp + bs], 6, axis=-1)]

        x1 = []
        for grp, (x, mods, tm, tpg) in enumerate(((xp, mods_p, tm_p, seq // tm_p), (xs, mods_s, tm_s, 1))):
            sh1, sc1, g1 = mods[0], mods[1], mods[2]
            hn = _prenorm(x, norm_mix[l], sc1, sh1, tm, tpg)
            (q,) = _proj(hn, w_in[l], 0, qk_w, [BF16], tm, tn, scale=scale, name="proj_q")
            k32, kbf = _proj(hn, w_in[l], qk_w, qk_w, [F32, BF16], tm, tn, name="proj_k")
            v32, vbf = _proj(hn, w_in[l], 2 * qk_w, qk_w, [F32, BF16], tm, tn, name="proj_v")
            (rg,) = _proj(hn, w_in[l], 3 * qk_w, 2 * d_rnn, [F32], tm, tn, name="proj_rnn")
            (gates,) = _proj(hn, w_in[l], 3 * qk_w + 2 * d_rnn, 2 * d, [F32], tm, tn, name="proj_gates")

            if grp == 0:
                nkb = seq // ATTN_BLOCK
                vt = vbf.reshape(bp, nkb, ATTN_BLOCK, n_heads, dv).transpose(0, 3, 1, 4, 2)
                vt = vt.reshape(bp * n_heads, nkb, dv, ATTN_BLOCK)
                o_att = _prompt_attention(q, kbf, vt, rel_bias_t, lam_vecs, subln_gain[l], bp, seq, lam_init,
                                          ATTN_BLOCK)
                hr, h_last = _rglru_prompt(rg, w_conv[l], b_conv[l], w_rg[l], b_rg[l], w_ig[l], b_ig[l], lam_rg[l],
                                           bp, seq)
                outs["kp"].append(k32.reshape(bp, seq, n_heads, dv))
                outs["vp"].append(v32.reshape(bp, seq, n_heads, dv))
                outs["cp"].append(rg.reshape(bp, seq, 2 * d_rnn)[:, seq - (w_conv.shape[1] - 1):, :d_rnn])
                outs["hp"].append(h_last.reshape(bp, d_rnn))
            else:
                dk = dv // 2
                q5 = q.reshape(bs, t_new, n_heads, 2, dk).transpose(0, 3, 2, 1, 4)
                zeros = jnp.zeros_like(q5[:, 0])
                qb = jnp.stack([jnp.concatenate([q5[:, 0], zeros], axis=-1),
                                jnp.concatenate([zeros, q5[:, 1]], axis=-1)], axis=1)
                qb = qb.reshape(bs, 2 * n_heads * t_new, dv)
                pad = ((0, 0), (0, PAGE_SIZE - t_new * n_heads), (0, 0))
                k_self = jnp.pad(kbf.reshape(bs, t_new * n_heads, dv), pad)
                v_self = jnp.pad(vbf.reshape(bs, t_new * n_heads, dv), pad)
                o_s = _sample_attention(qb, k_self, v_self, cache_k2, cache_v2, page_table, l * n_pool, rbt,
                                        lam_vecs, subln_gain[l], lam_init, n_heads, t_new)
                o_att = o_s.reshape(bs, n_heads, t_new, dv).transpose(0, 2, 1, 3).reshape(rs, n_heads * dv)
                x_t = rg[:, :d_rnn].reshape(bs, t_new, d_rnn).transpose(1, 0, 2)
                g_t = rg[:, d_rnn:].reshape(bs, t_new, d_rnn).transpose(1, 0, 2)
                conv_t = state_conv[l].transpose(1, 0, 2)
                hr_t, h_last = _rglru_sample(x_t, g_t, conv_t, state_h[l], w_conv[l], b_conv[l], w_rg[l], b_rg[l],
                                             w_ig[l], b_ig[l], lam_rg[l])
                hr = hr_t.transpose(1, 0, 2).reshape(rs, d_rnn)
                buf = jnp.concatenate([state_conv[l], rg[:, :d_rnn].reshape(bs, t_new, d_rnn)], axis=1)
                outs["ks"].append(k32.reshape(bs, t_new, n_heads, dv))
                outs["vs"].append(v32.reshape(bs, t_new, n_heads, dv))
                outs["cs"].append(buf[:, t_new:])
                outs["hs"].append(h_last)

            mixed = _mix(o_att, hr, w_att_out[l], w_rnn_out[l], gates, tm, tn)
            x1.append(_resid(mixed, w_o[l], x, g1, tm, tn, tpg))

        n_tok = rp + rs
        tpg2 = seq // tok_tile

        def per_tile(mp, ms):
            return jnp.concatenate([jnp.broadcast_to(mp, (bp, tok_tile, d)), ms], axis=0)

        sh2, sc2, g2 = (per_tile(mods_p[j], mods_s[j]) for j in (3, 4, 5))
        hn2, top_idx, gate, rank, counts = _router(x1[0], x1[1], norm_ffn[l], sc2, sh2, w_router[l], b_router[l],
                                                   tpg2)
        counts = counts[0]
        padded = (counts + expert_rows - 1) // expert_rows * expert_rows
        pad_end = jnp.cumsum(padded)
        pad_start = pad_end - padded
        dest = (pad_start[top_idx] + rank).reshape(-1)
        n_tiles = (n_tok * TOP_K) // expert_rows + n_exp
        n_used = (pad_end[-1] // expert_rows).astype(I32).reshape(1)
        tile_first = jnp.arange(n_tiles, dtype=I32) * expert_rows
        tile_expert = jnp.minimum(jnp.searchsorted(pad_end, tile_first, side="right"), n_exp - 1).astype(I32)
        tile_rows = jnp.clip(counts[tile_expert] - (tile_first - pad_start[tile_expert]), 0, expert_rows).astype(I32)
        src_token = jnp.zeros((n_tiles * expert_rows,), I32).at[dest].set(
            jnp.arange(n_tok * TOP_K, dtype=I32) // TOP_K)
        ys = _experts(hn2, tile_expert, tile_rows, n_used, src_token, w_up[l], b_up[l], w_down[l], b_down[l],
                      expert_rows, ff_tile, EXPERT_BIG_ROWS, GATHER_CHUNK)
        yp, ysm = _combine(dest.astype(I32), ys, x1[0], x1[1], gate, g2, final_norm, tpg2)
        xp, xs = yp, ysm

    return (xp.reshape(bp, seq, d), xs.reshape(bs, t_new, d),
            jnp.stack(outs["kp"]), jnp.stack(outs["vp"]), jnp.stack(outs["cp"]), jnp.stack(outs["hp"]),
            jnp.stack(outs["ks"]), jnp.stack(outs["vs"]), jnp.stack(outs["cs"]), jnp.stack(outs["hs"]))
```

```python
import functools
import math

import jax
import jax.numpy as jnp
import numpy as np
from jax import lax
from jax.experimental import pallas as pl
from jax.experimental.pallas import tpu as pltpu

F32, BF16, I32, U32 = jnp.float32, jnp.bfloat16, jnp.int32, jnp.uint32
HIGH_HALF = np.uint32(0xFFFF0000)

TOP_K = 4
RG_C = 8.0
SWIGLU_LIMIT = 7.0
SWIGLU_ALPHA = 1.702
NORM_EPS = 1e-6
MAX_DISTANCE = 128
PAGE_SIZE = 128
MASKED = -1e30

V7X_VMEM_LIMIT_BYTES = 56 * 1024 * 1024
SUBLANES = 8

PROMPT_ROW_TILE = 1024
WEIGHT_COL_TILE = 1024
EXPERT_ROW_TILE = 1280
EXPERT_BIG_ROWS = 1024
GATHER_CHUNK = 64
GATHER_UNROLL = 8
EXPERT_BLOCK_MIN_COST = 1200.0
EXPERT_BLOCK_ROW_COST = 6.6
FF_TILE = 256
ATTN_BLOCK = 256

NT_DIMS = (((1,), (1,)), ((), ()))


def _cparams(*sem):
    return pltpu.CompilerParams(dimension_semantics=sem, vmem_limit_bytes=V7X_VMEM_LIMIT_BYTES)


def _rms(x, gain):
    return x * lax.rsqrt(jnp.mean(x * x, axis=-1, keepdims=True) + NORM_EPS) * gain


def _gelu_tanh(x):
    return 0.5 * x * (1.0 + jnp.tanh(math.sqrt(2.0 / math.pi) * (x + 0.044715 * (x * x * x))))


def _softplus(x):
    return jnp.maximum(x, 0.0) + jnp.log1p(jnp.exp(-jnp.abs(x)))


def _t5_bucket(dist, n_buckets):
    n = jnp.maximum(dist, 0)
    max_exact = n_buckets // 2
    nf = jnp.maximum(n, 1).astype(F32)
    large = max_exact + (jnp.log(nf / max_exact) / math.log(MAX_DISTANCE / max_exact)
                         * (n_buckets - max_exact)).astype(I32)
    large = jnp.minimum(large, n_buckets - 1)
    return jnp.where(n < max_exact, n, large)


def _diff_lambda(lam_ref, lam_init):
    lq = lam_ref[...]
    s1 = jnp.sum(lq[0:1] * lq[1:2], axis=-1, keepdims=True)
    s2 = jnp.sum(lq[2:3] * lq[3:4], axis=-1, keepdims=True)
    return jnp.exp(s1) - jnp.exp(s2) + lam_init


def _ada_kernel(c_ref, w_ref, b_ref, o_ref):
    c = c_ref[...]
    s = (c * jax.nn.sigmoid(c)).astype(BF16)
    o_ref[...] = jnp.dot(s, w_ref[...].astype(BF16), preferred_element_type=F32) + b_ref[...]


def _ada(c, w, b, tn=WEIGHT_COL_TILE):
    m, d = c.shape
    n = w.shape[1]
    return pl.pallas_call(
        _ada_kernel,
        grid=(n // tn,),
        in_specs=[pl.BlockSpec((m, d), lambda j: (0, 0)),
                  pl.BlockSpec((d, tn), lambda j: (0, j)),
                  pl.BlockSpec((1, tn), lambda j: (0, j))],
        out_specs=pl.BlockSpec((m, tn), lambda j: (0, j)),
        out_shape=jax.ShapeDtypeStruct((m, n), F32),
        compiler_params=_cparams("arbitrary"),
        name="adaln",
    )(c, w, b.reshape(1, n))


def _prenorm_kernel(x_ref, g_ref, sc_ref, sh_ref, o_ref):
    y = _rms(x_ref[...], g_ref[...])
    o_ref[...] = (y * (1.0 + sc_ref[...]) + sh_ref[...]).astype(o_ref.dtype)


def _prenorm(x, gain, sc, sh, tm, tiles_per_group):
    r, d = x.shape
    rm = sc.shape[1]
    mod_spec = pl.BlockSpec((None, rm, d), lambda i: (i // tiles_per_group, 0, 0))
    return pl.pallas_call(
        _prenorm_kernel,
        grid=(r // tm,),
        in_specs=[pl.BlockSpec((tm, d), lambda i: (i, 0)),
                  pl.BlockSpec((1, d), lambda i: (0, 0)), mod_spec, mod_spec],
        out_specs=pl.BlockSpec((tm, d), lambda i: (i, 0)),
        out_shape=jax.ShapeDtypeStruct((r, d), BF16),
        compiler_params=_cparams("arbitrary"),
        name="prenorm_mix",
    )(x, gain.reshape(1, d), sc, sh)


def _proj_kernel(a_ref, w_ref, *refs, scale):
    outs, wbf = refs[:-1], refs[-1]

    @pl.when(pl.program_id(1) == 0)
    def _():
        wbf[...] = w_ref[...].astype(BF16)

    acc = jnp.dot(a_ref[...], wbf[...], preferred_element_type=F32)
    if scale != 1.0:
        acc = acc * scale
    for o in outs:
        o[...] = acc.astype(o.dtype)


def _proj(a, w, col_off, n_cols, out_dtypes, tm, tn, scale=1.0, name="proj"):
    m, k = a.shape
    off = col_off // tn
    out_spec = pl.BlockSpec((tm, tn), lambda j, i: (i, j))
    outs = pl.pallas_call(
        functools.partial(_proj_kernel, scale=scale),
        grid=(n_cols // tn, m // tm),
        in_specs=[pl.BlockSpec((tm, k), lambda j, i: (i, 0)),
                  pl.BlockSpec((k, tn), lambda j, i: (0, j + off))],
        out_specs=[out_spec] * len(out_dtypes),
        out_shape=[jax.ShapeDtypeStruct((m, n_cols), dt) for dt in out_dtypes],
        scratch_shapes=[pltpu.VMEM((k, tn), BF16)],
        compiler_params=_cparams("arbitrary", "arbitrary"),
        name=name,
    )(a, w)
    return outs


def _mix_kernel(a1_ref, a2_ref, w1_ref, w2_ref, ga_ref, gb_ref, o_ref, w1bf, w2bf):
    @pl.when(pl.program_id(1) == 0)
    def _():
        w1bf[...] = w1_ref[...].astype(BF16)
        w2bf[...] = w2_ref[...].astype(BF16)

    y1 = jnp.dot(a1_ref[...], w1bf[...], preferred_element_type=F32)
    y2 = jnp.dot(a2_ref[...], w2bf[...], preferred_element_type=F32)
    o_ref[...] = (jax.nn.sigmoid(ga_ref[...]) * y1 + jax.nn.sigmoid(gb_ref[...]) * y2).astype(o_ref.dtype)


def _mix(a1, a2, w1, w2, gates, tm, tn):
    m, k = a1.shape
    n = w1.shape[1]
    nb = n // tn
    return pl.pallas_call(
        _mix_kernel,
        grid=(nb, m // tm),
        in_specs=[pl.BlockSpec((tm, k), lambda j, i: (i, 0)),
                  pl.BlockSpec((tm, k), lambda j, i: (i, 0)),
                  pl.BlockSpec((k, tn), lambda j, i: (0, j)),
                  pl.BlockSpec((k, tn), lambda j, i: (0, j)),
                  pl.BlockSpec((tm, tn), lambda j, i: (i, j)),
                  pl.BlockSpec((tm, tn), lambda j, i: (i, j + nb))],
        out_specs=pl.BlockSpec((tm, tn), lambda j, i: (i, j)),
        out_shape=jax.ShapeDtypeStruct((m, n), BF16),
        scratch_shapes=[pltpu.VMEM((k, tn), BF16), pltpu.VMEM((k, tn), BF16)],
        compiler_params=_cparams("arbitrary", "arbitrary"),
        name="branch_mix",
    )(a1, a2, w1, w2, gates, gates)


def _resid_kernel(a_ref, w_ref, x_ref, g_ref, o_ref, wbf):
    @pl.when(pl.program_id(1) == 0)
    def _():
        wbf[...] = w_ref[...].astype(BF16)

    y = jnp.dot(a_ref[...], wbf[...], preferred_element_type=F32)
    o_ref[...] = x_ref[...] + g_ref[...] * y


def _resid(a, w, x, g, tm, tn, tiles_per_group):
    m, k = a.shape
    n = w.shape[1]
    rm = g.shape[1]
    return pl.pallas_call(
        _resid_kernel,
        grid=(n // tn, m // tm),
        in_specs=[pl.BlockSpec((tm, k), lambda j, i: (i, 0)),
                  pl.BlockSpec((k, tn), lambda j, i: (0, j)),
                  pl.BlockSpec((tm, tn), lambda j, i: (i, j)),
                  pl.BlockSpec((None, rm, tn), lambda j, i: (i // tiles_per_group, 0, j))],
        out_specs=pl.BlockSpec((tm, tn), lambda j, i: (i, j)),
        out_shape=jax.ShapeDtypeStruct((m, n), F32),
        scratch_shapes=[pltpu.VMEM((k, tn), BF16)],
        compiler_params=_cparams("arbitrary", "arbitrary"),
        name="out_proj_residual",
    )(a, w, x, g)


def _pattn_kernel(rb_ref, lam_ref, gain_ref, q_ref, k_ref, vt_ref, o_ref, bias_s, *, blk, lam_init, n_buckets, nq):
    h = pl.program_id(1)
    qi = pl.program_id(2)
    dk = q_ref.shape[1] // 2

    @pl.when(qi == 0)
    def _():
        r = lax.broadcasted_iota(I32, (blk, blk), 0)
        c = lax.broadcasted_iota(I32, (blk, blk), 1)
        far = rb_ref[h, n_buckets - 1]
        for which in range(2):
            dist = c - r + which * blk
            bkt = _t5_bucket(dist, n_buckets)
            val = jnp.zeros((blk, blk), F32)
            for b in range(n_buckets):
                val = jnp.where(bkt == b, rb_ref[h, b], val)
            val = val - far
            if which == 0:
                val = jnp.where(dist >= 0, val, MASKED)
            bias_s[which] = val

    q = q_ref[...]
    lane = lax.broadcasted_iota(I32, q.shape, 1)
    qmaps = (jnp.where(lane < dk, q, jnp.zeros_like(q)), jnp.where(lane >= dk, q, jnp.zeros_like(q)))

    lam = _diff_lambda(lam_ref, lam_init)

    def attend(n):
        ks = [k_ref[j * blk:(j + 1) * blk, :] for j in range(n)]
        vts = [vt_ref[j] for j in range(n)]
        biases = [None] * (n - 2) + [bias_s[1], bias_s[0]][-min(n, 2):]
        outs = []
        for mp in range(2):
            scores = []
            for k, bias in zip(ks, biases):
                s = lax.dot_general(k, qmaps[mp], NT_DIMS, preferred_element_type=F32)
                scores.append(s if bias is None else s + bias)
            smax = scores[0]
            for s in scores[1:]:
                smax = jnp.maximum(smax, s)
            m = jnp.max(smax, axis=0, keepdims=True)
            probs = [jnp.exp(s - m) for s in scores]
            psum = probs[0]
            for p in probs[1:]:
                psum = psum + p
            pv = None
            for p, vt in zip(probs, vts):
                d = jnp.dot(vt, p.astype(BF16), preferred_element_type=F32)
                pv = d if pv is None else pv + d
            outs.append(pv / jnp.sum(psum, axis=0, keepdims=True))
        ot = outs[0] - lam * outs[1]
        o_ref[...] = (_rms(ot.T, gain_ref[...]) * (1.0 - lam_init)).astype(o_ref.dtype)

    for n in range(1, nq + 1):
        pl.when(qi == n - 1)(functools.partial(attend, n))


def _prompt_attention(q, k, vt, rel_bias_t, lam_vecs, gain, batch, seq, lam_init, blk):
    n_heads, n_buckets = rel_bias_t.shape
    dv = vt.shape[2]
    assert blk >= MAX_DISTANCE and seq % blk == 0
    nq = seq // blk
    return pl.pallas_call(
        functools.partial(_pattn_kernel, blk=blk, lam_init=lam_init, n_buckets=n_buckets, nq=nq),
        grid=(batch, n_heads, nq),
        in_specs=[pl.BlockSpec(memory_space=pltpu.SMEM),
                  pl.BlockSpec(lam_vecs.shape, lambda b, h, i: (0, 0)),
                  pl.BlockSpec((1, dv), lambda b, h, i: (0, 0)),
                  pl.BlockSpec((blk, dv), lambda b, h, i: (b * nq + i, h)),
                  pl.BlockSpec((seq, dv), lambda b, h, i: (b, h)),
                  pl.BlockSpec((None, nq, dv, blk), lambda b, h, i: (b * n_heads + h, 0, 0, 0))],
        out_specs=pl.BlockSpec((blk, dv), lambda b, h, i: (b * nq + i, h)),
        out_shape=jax.ShapeDtypeStruct(q.shape, BF16),
        scratch_shapes=[pltpu.VMEM((2, blk, blk), F32)],
        compiler_params=_cparams("arbitrary", "arbitrary", "arbitrary"),
        name="prompt_diff_attention",
    )(rel_bias_t, lam_vecs, gain.reshape(1, dv), q, k, vt)


def _sattn_kernel(pt_ref, rbt_ref, lam_ref, gain_ref, qb_ref, ks_ref, vs_ref, ck_hbm, cv_hbm, o_ref,
                  kbuf, vbuf, mb_s, mbs_s, m_s, l_s, acc_s, sem,
                  *, group, n_pages, n_batch, page_off, lam_init, n_buckets, n_heads, t_new):
    b = pl.program_id(0)
    n_groups = n_pages // group
    rows = qb_ref.shape[0]
    half = rows // 2
    cols = kbuf.shape[2]
    scols = ks_ref.shape[0]

    @pl.when(b == 0)
    def _():
        def lookup(dist, width):
            bkt = _t5_bucket(dist, n_buckets)
            val = jnp.zeros(dist.shape, F32)
            for i in range(n_buckets):
                val = jnp.where(bkt == i, rbt_ref[i:i + 1, 0:width], val)
            return val - rbt_ref[n_buckets - 1:n_buckets, 0:width]

        r = lax.broadcasted_iota(I32, (rows, cols), 0)
        c = lax.broadcasted_iota(I32, (rows, cols), 1)
        hrow = (r % half) // t_new
        trow = r % t_new
        valid = hrow == c % n_heads
        mb_s[0] = jnp.where(valid, 0.0, MASKED)
        mb_s[1] = jnp.where(valid, lookup(PAGE_SIZE + trow - c // n_heads, cols), MASKED)
        r2 = lax.broadcasted_iota(I32, (rows, scols), 0)
        c2 = lax.broadcasted_iota(I32, (rows, scols), 1)
        t2 = r2 % t_new
        key2 = c2 // n_heads
        valid2 = ((r2 % half) // t_new == c2 % n_heads) & (key2 <= t2)
        mbs_s[...] = jnp.where(valid2, lookup(t2 - key2, scols), MASKED)

    def copies(bb, g, slot):
        out = []
        for j in range(group):
            page = pt_ref[bb, g * group + j] + page_off
            out.append(pltpu.make_async_copy(ck_hbm.at[page], kbuf.at[slot, j], sem.at[slot, 0, j]))
            out.append(pltpu.make_async_copy(cv_hbm.at[page], vbuf.at[slot, j], sem.at[slot, 1, j]))
        return out

    @pl.when(b == 0)
    def _():
        for cp in copies(0, 0, 0):
            cp.start()

    m_s[...] = jnp.full(m_s.shape, -jnp.inf, F32)
    l_s[...] = jnp.zeros(l_s.shape, F32)
    acc_s[...] = jnp.zeros(acc_s.shape, F32)
    qb = qb_ref[...]

    def update(blocks):
        scores = [lax.dot_general(qb, kp, NT_DIMS, preferred_element_type=F32) + mb for kp, _, mb in blocks]
        smax = scores[0]
        for s in scores[1:]:
            smax = jnp.maximum(smax, s)
        m_old = m_s[...]
        m_new = jnp.maximum(m_old, jnp.max(smax, axis=-1, keepdims=True))
        corr = jnp.exp(m_old - m_new)
        probs = [jnp.exp(s - m_new) for s in scores]
        psum = probs[0]
        for p in probs[1:]:
            psum = psum + p
        pv = None
        for p, (_, vp, _) in zip(probs, blocks):
            d = jnp.dot(p.astype(BF16), vp, preferred_element_type=F32)
            pv = d if pv is None else pv + d
        l_s[...] = l_s[...] * corr + jnp.sum(psum, axis=-1, keepdims=True)
        acc_s[...] = acc_s[...] * corr + pv
        m_s[...] = m_new

    @pl.loop(0, n_groups)
    def _(g):
        slot = g % 2
        nxt = g + 1

        @pl.when(nxt < n_groups)
        def _():
            for cp in copies(b, nxt, 1 - slot):
                cp.start()

        @pl.when((nxt == n_groups) & (b + 1 < n_batch))
        def _():
            for cp in copies(b + 1, 0, 1 - slot):
                cp.start()

        for cp in copies(b, g, slot):
            cp.wait()
        last = mb_s[jnp.where(g == n_groups - 1, 1, 0)]
        update([(kbuf[slot, j].astype(BF16), vbuf[slot, j].astype(BF16), last if j == group - 1 else mb_s[0])
                for j in range(group)])

    update([(ks_ref[...], vs_ref[...], mbs_s[...])])

    lam = _diff_lambda(lam_ref, lam_init)
    o = acc_s[...] / l_s[...]
    od = o[0:half] - lam * o[half:rows]
    o_ref[...] = (_rms(od, gain_ref[...]) * (1.0 - lam_init)).astype(o_ref.dtype)


def _sample_attention(qb, ks, vs, cache_k2, cache_v2, page_table, page_off, rbt, lam_vecs, gain, lam_init,
                      n_heads, t_new, group=8):
    n_batch, rows, dv = qb.shape
    n_pages = page_table.shape[1]
    cols = cache_k2.shape[1]
    n_buckets = rbt.shape[0]
    assert n_pages % group == 0 and (n_pages // group) % 2 == 0
    kern = functools.partial(_sattn_kernel, group=group, n_pages=n_pages, n_batch=n_batch, page_off=page_off,
                             lam_init=lam_init, n_buckets=n_buckets, n_heads=n_heads, t_new=t_new)
    grid_spec = pltpu.PrefetchScalarGridSpec(
        num_scalar_prefetch=1,
        grid=(n_batch,),
        in_specs=[pl.BlockSpec(rbt.shape, lambda b, pt: (0, 0)),
                  pl.BlockSpec(lam_vecs.shape, lambda b, pt: (0, 0)),
                  pl.BlockSpec((1, dv), lambda b, pt: (0, 0)),
                  pl.BlockSpec((None, rows, dv), lambda b, pt: (b, 0, 0)),
                  pl.BlockSpec((None,) + ks.shape[1:], lambda b, pt: (b, 0, 0)),
                  pl.BlockSpec((None,) + vs.shape[1:], lambda b, pt: (b, 0, 0)),
                  pl.BlockSpec(memory_space=pl.ANY),
                  pl.BlockSpec(memory_space=pl.ANY)],
        out_specs=pl.BlockSpec((None, rows // 2, dv), lambda b, pt: (b, 0, 0)),
        scratch_shapes=[pltpu.VMEM((2, group, cols, dv), F32), pltpu.VMEM((2, group, cols, dv), F32),
                        pltpu.VMEM((2, rows, cols), F32), pltpu.VMEM((rows, ks.shape[1]), F32),
                        pltpu.VMEM((rows, 1), F32), pltpu.VMEM((rows, 1), F32), pltpu.VMEM((rows, dv), F32),
                        pltpu.SemaphoreType.DMA((2, 2, group))])
    return pl.pallas_call(
        kern, grid_spec=grid_spec,
        out_shape=jax.ShapeDtypeStruct((n_batch, rows // 2, dv), BF16),
        compiler_params=_cparams("arbitrary"),
        name="sample_diff_attention",
    )(page_table, rbt, lam_vecs, gain.reshape(1, dv), qb, ks, vs, cache_k2, cache_v2)


def _rg_gates(xc, wrg_ref, brg_ref, wig_ref, big_ref, lam_ref):
    n_blocks, blk = wrg_ref.shape[0], wrg_ref.shape[1]
    rs, gs = [], []
    for n in range(n_blocks):
        xb = xc[:, n * blk:(n + 1) * blk].astype(BF16)
        rs.append(jnp.dot(xb, wrg_ref[n].astype(BF16), preferred_element_type=F32))
        gs.append(jnp.dot(xb, wig_ref[n].astype(BF16), preferred_element_type=F32))
    r = jax.nn.sigmoid(jnp.concatenate(rs, axis=-1) + brg_ref[...])
    i = jax.nn.sigmoid(jnp.concatenate(gs, axis=-1) + big_ref[...])
    log_a = (-RG_C * _softplus(-lam_ref[...])) * r
    a = jnp.exp(log_a)
    th = jnp.tanh(log_a)
    u = xc * i * jnp.sqrt(-2.0 * th / (1.0 - th))
    return a, u


def _rglru_prompt_kernel(x_ref, g_ref, wc_ref, bc_ref, wrg_ref, brg_ref, wig_ref, big_ref, lam_ref,
                         hr_ref, hl_ref, xbuf, a_s, u_s, h_s, *, conv_w):
    i = pl.program_id(1)
    tc = x_ref.shape[0]
    d = x_ref.shape[1]

    @pl.when(i == 0)
    def _():
        xbuf[0:SUBLANES] = jnp.zeros((SUBLANES, d), F32)
        h_s[...] = jnp.zeros(h_s.shape, F32)

    xbuf[SUBLANES:SUBLANES + tc] = x_ref[...]
    xc = bc_ref[...]
    for j in range(conv_w):
        lo = SUBLANES - (conv_w - 1) + j
        xc = xc + wc_ref[j:j + 1] * xbuf[lo:lo + tc]
    xbuf[0:SUBLANES] = xbuf[tc:tc + SUBLANES]

    a, u = _rg_gates(xc, wrg_ref, brg_ref, wig_ref, big_ref, lam_ref)
    a_s[...] = a
    u_s[...] = u
    row = lax.broadcasted_iota(I32, (SUBLANES, d), 0)

    @pl.loop(0, tc // SUBLANES)
    def _(g):
        st = pl.multiple_of(g * SUBLANES, SUBLANES)
        a8 = a_s[pl.ds(st, SUBLANES), :]
        u8 = u_s[pl.ds(st, SUBLANES), :]
        for s in (1, 2, 4):
            a_prev = jnp.where(row >= s, pltpu.roll(a8, s, 0), 1.0)
            u_prev = jnp.where(row >= s, pltpu.roll(u8, s, 0), 0.0)
            u8 = a8 * u_prev + u8
            a8 = a8 * a_prev
        h8 = a8 * h_s[...] + u8
        h_s[...] = jnp.broadcast_to(h8[SUBLANES - 1:SUBLANES], (SUBLANES, d))
        u_s[pl.ds(st, SUBLANES), :] = h8

    hr_ref[...] = (u_s[...] * _gelu_tanh(g_ref[...])).astype(hr_ref.dtype)

    @pl.when(i == pl.num_programs(1) - 1)
    def _():
        hl_ref[...] = h_s[0:1]


def _rglru_prompt(rg, w_conv, b_conv, w_rg, b_rg, w_ig, b_ig, lam_rg, batch, seq, tc=256):
    d = rg.shape[1] // 2
    nt = seq // tc
    conv_w = w_conv.shape[0]
    assert conv_w - 1 <= SUBLANES and seq % tc == 0
    vec = pl.BlockSpec((1, d), lambda b, i: (0, 0))
    wblk = pl.BlockSpec(w_rg.shape, lambda b, i: (0, 0, 0))
    return pl.pallas_call(
        functools.partial(_rglru_prompt_kernel, conv_w=conv_w),
        grid=(batch, nt),
        in_specs=[pl.BlockSpec((tc, d), lambda b, i: (b * nt + i, 0)),
                  pl.BlockSpec((tc, d), lambda b, i: (b * nt + i, 1)),
                  pl.BlockSpec((conv_w, d), lambda b, i: (0, 0)), vec, wblk, vec, wblk, vec, vec],
        out_specs=[pl.BlockSpec((tc, d), lambda b, i: (b * nt + i, 0)),
                   pl.BlockSpec((None, 1, d), lambda b, i: (b, 0, 0))],
        out_shape=[jax.ShapeDtypeStruct((batch * seq, d), BF16), jax.ShapeDtypeStruct((batch, 1, d), F32)],
        scratch_shapes=[pltpu.VMEM((tc + SUBLANES, d), F32), pltpu.VMEM((tc, d), F32),
                        pltpu.VMEM((tc, d), F32), pltpu.VMEM((SUBLANES, d), F32)],
        compiler_params=_cparams("arbitrary", "arbitrary"),
        name="rglru_prompt",
    )(rg, rg, w_conv, b_conv.reshape(1, d), w_rg, b_rg.reshape(1, d), w_ig, b_ig.reshape(1, d),
      lam_rg.reshape(1, d))


def _rglru_sample_kernel(x_ref, g_ref, cs_ref, h0_ref, wc_ref, bc_ref, wrg_ref, brg_ref, wig_ref, big_ref,
                         lam_ref, hr_ref, hl_ref, *, conv_w):
    t_new = x_ref.shape[0]
    buf = [cs_ref[j] for j in range(conv_w - 1)] + [x_ref[t] for t in range(t_new)]
    h = h0_ref[...]
    for t in range(t_new):
        xc = bc_ref[...]
        for j in range(conv_w):
            xc = xc + wc_ref[j:j + 1] * buf[t + j]
        a, u = _rg_gates(xc, wrg_ref, brg_ref, wig_ref, big_ref, lam_ref)
        h = a * h + u
        hr_ref[t] = (h * _gelu_tanh(g_ref[t])).astype(hr_ref.dtype)
    hl_ref[...] = h


def _rglru_sample(x_t, g_t, conv_t, h0, w_conv, b_conv, w_rg, b_rg, w_ig, b_ig, lam_rg):
    t_new, batch, d = x_t.shape
    return pl.pallas_call(
        functools.partial(_rglru_sample_kernel, conv_w=w_conv.shape[0]),
        out_shape=[jax.ShapeDtypeStruct((t_new, batch, d), BF16), jax.ShapeDtypeStruct((batch, d), F32)],
        compiler_params=pltpu.CompilerParams(vmem_limit_bytes=V7X_VMEM_LIMIT_BYTES),
        name="rglru_sample",
    )(x_t, g_t, conv_t, h0, w_conv, b_conv.reshape(1, d), w_rg, b_rg.reshape(1, d), w_ig, b_ig.reshape(1, d),
      lam_rg.reshape(1, d))


def _router_kernel(xp_ref, xs_ref, gain_ref, sc_ref, sh_ref, wr_ref, br_ref, hn_ref, idx_ref, gate_ref, rank_ref,
                   cnt_ref, base_s, *, n_prompt_tiles):
    i = pl.program_id(0)
    tm = xp_ref.shape[0]
    n_exp = wr_ref.shape[1]

    @pl.when(i == 0)
    def _():
        base_s[...] = jnp.zeros(base_s.shape, F32)

    x = jnp.where(i < n_prompt_tiles, xp_ref[...], xs_ref[...])
    hn = _rms(x, gain_ref[...]) * (1.0 + sc_ref[...]) + sh_ref[...]
    hb = hn.astype(BF16)
    bits = lax.bitcast_convert_type(hb.astype(F32), U32)
    half = bits.shape[1] // 2
    hn_ref[...] = (bits[:, half:] & HIGH_HALF) | (bits[:, :half] >> 16)
    logits = jnp.dot(hb, wr_ref[...].astype(BF16), preferred_element_type=F32) + br_ref[...]

    lane = lax.broadcasted_iota(I32, (tm, n_exp), 1)
    work = logits
    sels, vals = [], []
    for _ in range(TOP_K):
        mx = jnp.max(work, axis=-1, keepdims=True)
        first = jnp.min(jnp.where(work == mx, lane, n_exp), axis=-1, keepdims=True)
        sel = lane == first
        sels.append(sel)
        vals.append(mx)
        idx_ref[:, len(sels) - 1:len(sels)] = first
        work = jnp.where(sel, -jnp.inf, work)

    es = [jnp.exp(v - vals[0]) for v in vals]
    tot = es[0]
    for e in es[1:]:
        tot = tot + e
    for k in range(TOP_K):
        gate_ref[:, k:k + 1] = es[k] / tot

    chosen = sels[0]
    for sel in sels[1:]:
        chosen = chosen | sel
    onehot = jnp.where(chosen, 1.0, 0.0)
    r = lax.broadcasted_iota(I32, (tm, tm), 0)
    c = lax.broadcasted_iota(I32, (tm, tm), 1)
    tri = jnp.where(r > c, 1.0, 0.0).astype(BF16)
    pos = jnp.dot(tri, onehot.astype(BF16), preferred_element_type=F32) + base_s[...]
    for k in range(TOP_K):
        rank_ref[:, k:k + 1] = jnp.sum(jnp.where(sels[k], pos, 0.0), axis=-1, keepdims=True).astype(I32)
    base_s[...] = base_s[...] + jnp.sum(onehot, axis=0, keepdims=True)
    cnt_ref[...] = base_s[...].astype(I32)


def _router(x1p, x1s, gain, sc, sh, w_router, b_router, tiles_per_group):
    tm, d = x1s.shape
    npt = x1p.shape[0] // tm
    r = x1p.shape[0] + tm
    n_exp = w_router.shape[1]
    n_groups = sc.shape[0]
    mod_spec = pl.BlockSpec((None, tm, d), lambda i: (jnp.minimum(i // tiles_per_group, n_groups - 1), 0, 0))
    small = pl.BlockSpec((tm, TOP_K), lambda i: (i, 0))
    return pl.pallas_call(
        functools.partial(_router_kernel, n_prompt_tiles=npt),
        grid=(r // tm,),
        in_specs=[pl.BlockSpec((tm, d), lambda i: (jnp.minimum(i, npt - 1), 0)),
                  pl.BlockSpec((tm, d), lambda i: (0, 0)), pl.BlockSpec((1, d), lambda i: (0, 0)),
                  mod_spec, mod_spec,
                  pl.BlockSpec((d, n_exp), lambda i: (0, 0)), pl.BlockSpec((1, n_exp), lambda i: (0, 0))],
        out_specs=[pl.BlockSpec((tm, d // 2), lambda i: (i, 0)), small, small, small,
                   pl.BlockSpec((1, n_exp), lambda i: (0, 0))],
        out_shape=[jax.ShapeDtypeStruct((r, d // 2), U32), jax.ShapeDtypeStruct((r, TOP_K), I32),
                   jax.ShapeDtypeStruct((r, TOP_K), F32), jax.ShapeDtypeStruct((r, TOP_K), I32),
                   jax.ShapeDtypeStruct((1, n_exp), I32)],
        scratch_shapes=[pltpu.VMEM((1, n_exp), F32)],
        compiler_params=_cparams("arbitrary"),
        name="router_topk",
    )(x1p, x1s, gain.reshape(1, d), sc, sh, w_router, b_router.reshape(1, n_exp))


@functools.lru_cache(maxsize=None)
def _row_block_plan(units, cap, sizes, unit_rows):
    def cost(sz):
        return max(EXPERT_BLOCK_MIN_COST, EXPERT_BLOCK_ROW_COST * sz * unit_rows)

    @functools.lru_cache(maxsize=None)
    def solve(r, room):
        if r <= 0:
            return 0.0, ()
        best = None
        for sz in sizes:
            if sz <= room:
                c, p = solve(r - sz, room - sz)
                if best is None or cost(sz) + c < best[0]:
                    best = (cost(sz) + c, (sz,) + p)
        return best

    return solve(units, cap)[1]


def _expert_kernel(te_ref, rows_ref, nu_ref, src_ref, hn_hbm, wg_ref, wu_ref, bg_ref, bu_ref, wd_ref, bd_ref, o_ref,
                   xg, xb, wgb, wub, wdb, sem, *, big, chunk):
    t = pl.program_id(0)
    f = pl.program_id(1)
    tb = xb.shape[0]
    n_used = nu_ref[0]

    def n_chunks(tile):
        return (rows_ref[tile] + chunk - 1) // chunk

    def gather(tile):
        base = tile * tb

        @pl.loop(0, n_chunks(tile) * (chunk // GATHER_UNROLL))
        def _(i):
            r0 = i * GATHER_UNROLL
            for j in range(GATHER_UNROLL):
                tok = src_ref[base + r0 + j]
                pltpu.make_async_copy(hn_hbm.at[pl.ds(tok, 1)], xg.at[pl.ds(r0 + j, 1)], sem.at[0]).start()

    @pl.when((t == 0) & (f == 0))
    def _():
        xb[...] = jnp.zeros(xb.shape, BF16)
        gather(0)

    @pl.when((t < n_used) & (f == 0))
    def _():
        @pl.loop(0, n_chunks(t))
        def _(c):
            pltpu.make_async_copy(hn_hbm.at[pl.ds(0, chunk)], xg.at[pl.ds(0, chunk)], sem.at[0]).wait()

        half = xg.shape[1]

        @pl.loop(0, n_chunks(t))
        def _(c):
            rs = pl.ds(pl.multiple_of(c * chunk, chunk), chunk)
            w = xg[rs, :]
            xb[rs, 0:half] = lax.bitcast_convert_type(w << 16, F32).astype(BF16)
            xb[rs, half:2 * half] = lax.bitcast_convert_type(w & HIGH_HALF, F32).astype(BF16)

        o_ref[...] = jnp.broadcast_to(bd_ref[...], o_ref.shape)

        @pl.when(t + 1 < n_used)
        def _():
            gather(t + 1)

    @pl.when(t < n_used)
    def _():
        wgb[...] = wg_ref[...].astype(BF16)
        wub[...] = wu_ref[...].astype(BF16)
        wdb[...] = wd_ref[...].astype(BF16)

        def ffn(start, size):
            rs = pl.ds(pl.multiple_of(start, chunk), size)
            x = xb[rs, :]
            g = jnp.dot(x, wgb[...], preferred_element_type=F32) + bg_ref[...]
            u = jnp.dot(x, wub[...], preferred_element_type=F32) + bu_ref[...]
            g = jnp.minimum(g, SWIGLU_LIMIT)
            u = jnp.clip(u, -SWIGLU_LIMIT, SWIGLU_LIMIT)
            act = g * jax.nn.sigmoid(SWIGLU_ALPHA * g) * (u + 1.0)
            o_ref[rs, :] += jnp.dot(act.astype(BF16), wdb[...], preferred_element_type=F32)

        units = (rows_ref[t] + chunk - 1) // chunk
        big_u, tb_u = big // chunk, tb // chunk
        n_big = units // big_u
        rem = units - n_big * big_u
        sizes = []
        size = big
        while size >= chunk:
            sizes.append(size)
            size //= 2
        counts = {size: jnp.where(size == big, n_big, 0) for size in sizes}
        for nb in range(tb_u // big_u + 1):
            cap = min(big_u, tb_u - nb * big_u)
            for r in range(1, cap + 1):
                plan = _row_block_plan(r, cap, tuple(sz // chunk for sz in sizes), chunk)
                for size in sizes:
                    extra = plan.count(size // chunk)
                    if extra:
                        hit = (n_big == nb) & (rem == r)
                        counts[size] = jnp.where(hit, (nb if size == big else 0) + extra, counts[size])
        done = 0
        for size in sizes:
            @pl.loop(0, counts[size])
            def _(i, done=done, size=size):
                ffn(done + i * size, size)

            done = done + counts[size] * size

    @pl.when((t >= n_used) & (f == 0))
    def _():
        o_ref[...] = jnp.zeros(o_ref.shape, F32)


def _experts(hn, tile_expert, tile_rows, n_used, src_token, w_up, b_up, w_down, b_down, tb, tf, big, chunk):
    n_exp, d, ff2 = w_up.shape
    ff = ff2 // 2
    nf = ff // tf
    n_tiles = tile_expert.shape[0]
    assert tb % chunk == 0 and big % chunk == 0 and chunk % GATHER_UNROLL == 0

    def eff(t, f, nu):
        return jnp.minimum(t, nu[0] - 1), jnp.where(t < nu[0], f, nf - 1)

    def wg_map(t, f, te, rows, nu, src):
        tt, fe = eff(t, f, nu)
        return te[tt], 0, fe

    def wu_map(t, f, te, rows, nu, src):
        tt, fe = eff(t, f, nu)
        return te[tt], 0, fe + nf

    def wd_map(t, f, te, rows, nu, src):
        tt, fe = eff(t, f, nu)
        return te[tt], fe, 0

    def bd_map(t, f, te, rows, nu, src):
        tt, _ = eff(t, f, nu)
        return te[tt], 0, 0

    def out_map(t, f, te, rows, nu, src):
        return t, 0

    grid_spec = pltpu.PrefetchScalarGridSpec(
        num_scalar_prefetch=4,
        grid=(n_tiles, nf),
        in_specs=[pl.BlockSpec(memory_space=pl.ANY),
                  pl.BlockSpec((None, d, tf), wg_map), pl.BlockSpec((None, d, tf), wu_map),
                  pl.BlockSpec((None, 1, tf), wg_map), pl.BlockSpec((None, 1, tf), wu_map),
                  pl.BlockSpec((None, tf, d), wd_map), pl.BlockSpec((None, 1, d), bd_map)],
        out_specs=pl.BlockSpec((tb, d), out_map),
        scratch_shapes=[pltpu.VMEM((tb, d // 2), U32), pltpu.VMEM((tb, d), BF16),
                        pltpu.VMEM((d, tf), BF16), pltpu.VMEM((d, tf), BF16), pltpu.VMEM((tf, d), BF16),
                        pltpu.SemaphoreType.DMA((1,))])
    return pl.pallas_call(
        functools.partial(_expert_kernel, big=big, chunk=chunk), grid_spec=grid_spec,
        out_shape=jax.ShapeDtypeStruct((n_tiles * tb, d), F32),
        compiler_params=_cparams("arbitrary", "arbitrary"),
        name="routed_experts",
    )(tile_expert, tile_rows, n_used, src_token, hn, w_up, w_up, b_up.reshape(n_exp, 1, ff2),
      b_up.reshape(n_exp, 1, ff2), w_down, b_down.reshape(n_exp, 1, d))


def _combine_kernel(dest_ref, ys_hbm, xp_ref, xs_ref, gate_ref, g2_ref, fn_ref, yp_ref, ysm_ref, buf, sem,
                    *, n_prompt_tiles):
    i = pl.program_id(0)
    n = pl.num_programs(0)
    tm = xp_ref.shape[0]
    slot = i % 2

    def gather(tile, into):
        base = tile * tm * TOP_K
        per_iter = GATHER_UNROLL // TOP_K

        @pl.loop(0, tm // per_iter)
        def _(it):
            for j in range(per_iter):
                r = it * per_iter + j
                for k in range(TOP_K):
                    row = dest_ref[base + r * TOP_K + k]
                    pltpu.make_async_copy(ys_hbm.at[pl.ds(row, 1)], buf.at[into, k, pl.ds(r, 1)],
                                          sem.at[into]).start(priority=(j * TOP_K + k) % 2)

    @pl.when(i == 0)
    def _():
        gather(0, 0)

    @pl.when(i + 1 < n)
    def _():
        gather(i + 1, 1 - slot)

    for k in range(TOP_K):
        pltpu.make_async_copy(ys_hbm.at[pl.ds(0, tm)], buf.at[slot, k], sem.at[slot]).wait()

    moe = gate_ref[:, 0:1] * buf[slot, 0]
    for k in range(1, TOP_K):
        moe = moe + gate_ref[:, k:k + 1] * buf[slot, k]
    x = jnp.where(i < n_prompt_tiles, xp_ref[...], xs_ref[...])
    y = _rms(x + g2_ref[...] * moe, fn_ref[...])

    @pl.when(i < n_prompt_tiles)
    def _():
        yp_ref[...] = y

    @pl.when(i >= n_prompt_tiles)
    def _():
        ysm_ref[...] = y


def _combine(dest_flat, ys, x1p, x1s, gate, g2, final_norm, tiles_per_group):
    tm, d = x1s.shape
    n_prompt_rows = x1p.shape[0]
    r = n_prompt_rows + tm
    n_groups = g2.shape[0]
    npt = n_prompt_rows // tm
    grid_spec = pltpu.PrefetchScalarGridSpec(
        num_scalar_prefetch=1,
        grid=(r // tm,),
        in_specs=[pl.BlockSpec(memory_space=pl.ANY),
                  pl.BlockSpec((tm, d), lambda i, dst: (jnp.minimum(i, npt - 1), 0)),
                  pl.BlockSpec((tm, d), lambda i, dst: (0, 0)),
                  pl.BlockSpec((tm, TOP_K), lambda i, dst: (i, 0)),
                  pl.BlockSpec((None, tm, d), lambda i, dst: (jnp.minimum(i // tiles_per_group, n_groups - 1), 0, 0)),
                  pl.BlockSpec((1, d), lambda i, dst: (0, 0))],
        out_specs=[pl.BlockSpec((tm, d), lambda i, dst: (jnp.minimum(i, npt - 1), 0)),
                   pl.BlockSpec((tm, d), lambda i, dst: (0, 0))],
        scratch_shapes=[pltpu.VMEM((2, TOP_K, tm, d), F32), pltpu.SemaphoreType.DMA((2,))])
    return pl.pallas_call(
        functools.partial(_combine_kernel, n_prompt_tiles=npt), grid_spec=grid_spec,
        out_shape=[jax.ShapeDtypeStruct((n_prompt_rows, d), F32), jax.ShapeDtypeStruct((tm, d), F32)],
        compiler_params=_cparams("arbitrary"),
        name="expert_combine_final_norm",
    )(dest_flat, ys, x1p, x1s, gate, g2, final_norm.reshape(1, d))


def kernel(x_prompt, x_sample, c_prompt, c_sample, cache_k, cache_v, state_conv, state_h, page_table, rel_bias, w_ada, b_ada, norm_mix, norm_ffn, w_in, lam_q1, lam_k1, lam_q2, lam_k2, subln_gain, w_conv, b_conv, w_rg, b_rg, w_ig, b_ig, lam_rg, w_att_out, w_rnn_out, w_o, w_router, b_router, w_up, b_up, w_down, b_down, final_norm):
    bp, seq, d = x_prompt.shape
    bs, t_new, _ = x_sample.shape
    depth, n_pool, page, n_heads, dv = cache_v.shape
    assert page == PAGE_SIZE and cache_k.shape[-1] == dv
    d_rnn = w_conv.shape[-1]
    qk_w = n_heads * dv
    n_exp = w_router.shape[-1]
    rp, rs = bp * seq, bs * t_new
    scale = (dv // 2) ** -0.5
    assert depth == 1
    tm_p, tm_s = PROMPT_ROW_TILE, rs
    tok_tile = rs
    assert seq % tm_p == 0 and rp % tok_tile == 0 and rs % SUBLANES == 0
    expert_rows, ff_tile, tn = EXPERT_ROW_TILE, FF_TILE, WEIGHT_COL_TILE

    xp = x_prompt.reshape(rp, d)
    xs = x_sample.reshape(rs, d)
    c_all = jnp.concatenate([c_prompt, c_sample, jnp.zeros((-(bp + bs) % SUBLANES, d), F32)], axis=0)
    cache_k2 = cache_k.reshape(depth * n_pool, page * n_heads, dv)
    cache_v2 = cache_v.reshape(depth * n_pool, page * n_heads, dv)
    rel_bias_t = rel_bias.T
    rbt = jnp.tile(rel_bias, (1, page))

    outs = {k: [] for k in ("kp", "vp", "cp", "hp", "ks", "vs", "cs", "hs")}
    for l in range(depth):
        lam_init = 0.8 - 0.6 * math.exp(-0.3 * l)
        lam_vecs = jnp.stack([lam_q1[l], lam_k1[l], lam_q2[l], lam_k2[l]])
        mod = _ada(c_all, w_ada[l], b_ada[l])
        mods_p = [m.reshape(bp, 1, d) for m in jnp.split(mod[:bp], 6, axis=-1)]
        mods_s = [jnp.repeat(m, t_new, axis=0).reshape(1, rs, d) for m in jnp.split(mod[bp:bp + bs], 6, axis=-1)]

        x1 = []
        for grp, (x, mods, tm, tpg) in enumerate(((xp, mods_p, tm_p, seq // tm_p), (xs, mods_s, tm_s, 1))):
            sh1, sc1, g1 = mods[0], mods[1], mods[2]
            hn = _prenorm(x, norm_mix[l], sc1, sh1, tm, tpg)
            (q,) = _proj(hn, w_in[l], 0, qk_w, [BF16], tm, tn, scale=scale, name="proj_q")
            k32, kbf = _proj(hn, w_in[l], qk_w, qk_w, [F32, BF16], tm, tn, name="proj_k")
            v32, vbf = _proj(hn, w_in[l], 2 * qk_w, qk_w, [F32, BF16], tm, tn, name="proj_v")
            (rg,) = _proj(hn, w_in[l], 3 * qk_w, 2 * d_rnn, [F32], tm, tn, name="proj_rnn")
            (gates,) = _proj(hn, w_in[l], 3 * qk_w + 2 * d_rnn, 2 * d, [F32], tm, tn, name="proj_gates")

            if grp == 0:
                nkb = seq // ATTN_BLOCK
                vt = vbf.reshape(bp, nkb, ATTN_BLOCK, n_heads, dv).transpose(0, 3, 1, 4, 2)
                vt = vt.reshape(bp * n_heads, nkb, dv, ATTN_BLOCK)
                o_att = _prompt_attention(q, kbf, vt, rel_bias_t, lam_vecs, subln_gain[l], bp, seq, lam_init,
                                          ATTN_BLOCK)
                hr, h_last = _rglru_prompt(rg, w_conv[l], b_conv[l], w_rg[l], b_rg[l], w_ig[l], b_ig[l], lam_rg[l],
                                           bp, seq)
                outs["kp"].append(k32.reshape(bp, seq, n_heads, dv))
                outs["vp"].append(v32.reshape(bp, seq, n_heads, dv))
                outs["cp"].append(rg.reshape(bp, seq, 2 * d_rnn)[:, seq - (w_conv.shape[1] - 1):, :d_rnn])
                outs["hp"].append(h_last.reshape(bp, d_rnn))
            else:
                dk = dv // 2
                q5 = q.reshape(bs, t_new, n_heads, 2, dk).transpose(0, 3, 2, 1, 4)
                zeros = jnp.zeros_like(q5[:, 0])
                qb = jnp.stack([jnp.concatenate([q5[:, 0], zeros], axis=-1),
                                jnp.concatenate([zeros, q5[:, 1]], axis=-1)], axis=1)
                qb = qb.reshape(bs, 2 * n_heads * t_new, dv)
                pad = ((0, 0), (0, PAGE_SIZE - t_new * n_heads), (0, 0))
                k_self = jnp.pad(kbf.reshape(bs, t_new * n_heads, dv), pad)
                v_self = jnp.pad(vbf.reshape(bs, t_new * n_heads, dv), pad)
                o_s = _sample_attention(qb, k_self, v_self, cache_k2, cache_v2, page_table, l * n_pool, rbt,
                                        lam_vecs, subln_gain[l], lam_init, n_heads, t_new)
                o_att = o_s.reshape(bs, n_heads, t_new, dv).transpose(0, 2, 1, 3).reshape(rs, n_heads * dv)
                x_t = rg[:, :d_rnn].reshape(bs, t_new, d_rnn).transpose(1, 0, 2)
                g_t = rg[:, d_rnn:].reshape(bs, t_new, d_rnn).transpose(1, 0, 2)
                conv_t = state_conv[l].transpose(1, 0, 2)
                hr_t, h_last = _rglru_sample(x_t, g_t, conv_t, state_h[l], w_conv[l], b_conv[l], w_rg[l], b_rg[l],
                                             w_ig[l], b_ig[l], lam_rg[l])
                hr = hr_t.transpose(1, 0, 2).reshape(rs, d_rnn)
                buf = jnp.concatenate([state_conv[l], rg[:, :d_rnn].reshape(bs, t_new, d_rnn)], axis=1)
                outs["ks"].append(k32.reshape(bs, t_new, n_heads, dv))
                outs["vs"].append(v32.reshape(bs, t_new, n_heads, dv))
                outs["cs"].append(buf[:, t_new:])
                outs["hs"].append(h_last)

            mixed = _mix(o_att, hr, w_att_out[l], w_rnn_out[l], gates, tm, tn)
            x1.append(_resid(mixed, w_o[l], x, g1, tm, tn, tpg))

        n_tok = rp + rs
        tpg2 = seq // tok_tile

        def per_tile(mp, ms):
            return jnp.concatenate([jnp.broadcast_to(mp, (bp, tok_tile, d)), ms], axis=0)

        sh2, sc2, g2 = (per_tile(mods_p[j], mods_s[j]) for j in (3, 4, 5))
        hn2, top_idx, gate, rank, counts = _router(x1[0], x1[1], norm_ffn[l], sc2, sh2, w_router[l], b_router[l],
                                                   tpg2)
        counts = counts[0]
        padded = (counts + expert_rows - 1) // expert_rows * expert_rows
        pad_end = jnp.cumsum(padded)
        pad_start = pad_end - padded
        dest = (pad_start[top_idx] + rank).reshape(-1)
        n_tiles = (n_tok * TOP_K) // expert_rows + n_exp
        n_used = (pad_end[-1] // expert_rows).astype(I32).reshape(1)
        tile_first = jnp.arange(n_tiles, dtype=I32) * expert_rows
        tile_expert = jnp.minimum(jnp.searchsorted(pad_end, tile_first, side="right"), n_exp - 1).astype(I32)
        tile_rows = jnp.clip(counts[tile_expert] - (tile_first - pad_start[tile_expert]), 0, expert_rows).astype(I32)
        src_token = jnp.zeros((n_tiles * expert_rows,), I32).at[dest].set(
            jnp.arange(n_tok * TOP_K, dtype=I32) // TOP_K)
        ys = _experts(hn2, tile_expert, tile_rows, n_used, src_token, w_up[l], b_up[l], w_down[l], b_down[l],
                      expert_rows, ff_tile, EXPERT_BIG_ROWS, GATHER_CHUNK)
        yp, ysm = _combine(dest.astype(I32), ys, x1[0], x1[1], gate, g2, final_norm, tpg2)
        xp, xs = yp, ysm

    return (xp.reshape(bp, seq, d), xs.reshape(bs, t_new, d),
            jnp.stack(outs["kp"]), jnp.stack(outs["vp"]), jnp.stack(outs["cp"]), jnp.stack(outs["hp"]),
            jnp.stack(outs["ks"]), jnp.stack(outs["vs"]), jnp.stack(outs["cs"]), jnp.stack(outs["hs"]))
```
